```python
import jax
import jax.numpy as jnp
from jax import lax
import numpy as np

D_MODEL = 1024
BATCH = 4
SEQ = 8192
DEPTH = 2

N_MIXERS = 4
GROUP_WIDTH = D_MODEL // N_MIXERS
HEAD_DIM = 64
N_HEADS = GROUP_WIDTH // HEAD_DIM
D_FF = ((8 * D_MODEL // 3 + 127) // 128) * 128
CONV_WIDTH = 3
CHUNK = 128
ROPE_BASE = 10000.0
RET_DECAY_EXP0 = 5
RWKV_W_RANK = 32
RWKV_A_RANK = 32
RWKV_G_RANK = 64
RWKV_COLS = 3 * GROUP_WIDTH + 2 * RWKV_W_RANK + 2 * RWKV_A_RANK + RWKV_G_RANK
MLSTM_COLS = 4 * GROUP_WIDTH + 4 * N_HEADS
P_TOTAL = 3 * GROUP_WIDTH + 4 * GROUP_WIDTH + RWKV_COLS + MLSTM_COLS
RMS_EPS = 1e-6
RWKV_GN_EPS = 64e-5

kernel_name = 'hybrid_parallel_mixer_encoder'


def split_cols(t, widths):
    points, acc = [], 0
    for w in widths[:-1]:
        acc += w
        points.append(acc)
    return jnp.split(t, points, axis=-1)


def rms_norm(x, g):
    xf = x.astype(jnp.float32)
    y = xf * lax.rsqrt(jnp.mean(xf * xf, axis=-1, keepdims=True) + RMS_EPS)
    return (y * g.astype(jnp.float32)).astype(x.dtype)


def swiglu(h, w_gate, w_up, w_down):
    return (jax.nn.silu(h @ w_gate) * (h @ w_up)) @ w_down


def heads(t):
    b, s, _ = t.shape
    return t.reshape(b, s, N_HEADS, HEAD_DIM).transpose(0, 2, 1, 3)


def merge_heads(t):
    b, h, s, d = t.shape
    return t.transpose(0, 2, 1, 3).reshape(b, s, h * d)


def to_chunks(t):
    b, h, s = t.shape[:3]
    t = t.reshape(b, h, s // CHUNK, CHUNK, *t.shape[3:])
    return jnp.moveaxis(t, 2, 0)


def from_chunks(t):
    t = jnp.moveaxis(t, 0, 2)
    b, h, nc, l = t.shape[:4]
    return t.reshape(b, h, nc * l, *t.shape[4:])


def rope_tables(seq):
    inv = ROPE_BASE ** (-jnp.arange(0, HEAD_DIM, 2, dtype=jnp.float32) / HEAD_DIM)
    ang = jnp.arange(seq, dtype=jnp.float32)[:, None] * inv[None, :]
    return jnp.cos(ang), jnp.sin(ang)


def rotary(t, cos, sin):
    t1, t2 = jnp.split(t, 2, axis=-1)
    return jnp.concatenate([t1 * cos - t2 * sin, t1 * sin + t2 * cos], axis=-1)


def short_conv_mixer(zb, zc, zh, conv_w, conv_b):
    u = zc * zh
    y = lax.conv_general_dilated(
        u, conv_w[:, None, :].astype(u.dtype), window_strides=(1,),
        padding=[(CONV_WIDTH // 2, CONV_WIDTH // 2)],
        dimension_numbers=('NWC', 'WIO', 'NWC'), feature_group_count=GROUP_WIDTH)
    return zb * (y + conv_b.astype(y.dtype))


def retention_scan(q, k, v, log_gamma, include_diag):
    idx = jnp.arange(CHUNK, dtype=jnp.float32)
    rel = idx[:, None] - idx[None, :]
    mask = (rel >= 0) if include_diag else (rel > 0)
    intra = jnp.where(mask, jnp.exp(jnp.where(mask, rel, 0.0) * log_gamma[:, None, None]), 0.0)
    q_dec = jnp.exp((idx + 1.0) * log_gamma[:, None])
    k_dec = jnp.exp((CHUNK - 1.0 - idx) * log_gamma[:, None])
    c_dec = jnp.exp(CHUNK * log_gamma)

    def step(state, inp):
        qc, kc, vc = inp
        s = jnp.einsum('bhtd,bhsd->bhts', qc, kc) * intra
        out = (jnp.einsum('bhts,bhsv->bhtv', s, vc)
               + q_dec[:, :, None] * jnp.einsum('bhtk,bhkv->bhtv', qc, state))
        state = c_dec[:, None, None] * state + jnp.einsum('bhsk,bhsv->bhkv', kc * k_dec[:, :, None], vc)
        return state, out

    b, h, _, d = q.shape
    state0 = jnp.zeros((b, h, d, d), jnp.float32)
    _, out = lax.scan(step, state0, (to_chunks(q), to_chunks(k), to_chunks(v)))
    return from_chunks(out)


def retention_mixer(zq, zk, zv, zg, decay_logit, cos, sin):
    f32 = jnp.float32
    q = rotary(heads(zq.astype(f32)), cos, sin) * HEAD_DIM ** -0.5
    k = rotary(heads(zk.astype(f32)), cos, sin)
    v = heads(zv.astype(f32))
    log_gamma = jax.nn.log_sigmoid(decay_logit.astype(f32))
    fwd = retention_scan(q, k, v, log_gamma[0], True)
    bwd = jnp.flip(retention_scan(jnp.flip(q, 2), jnp.flip(k, 2), jnp.flip(v, 2), log_gamma[1], False), 2)
    o = fwd + bwd
    o = o * lax.rsqrt(jnp.mean(o * o, axis=-1, keepdims=True) + RMS_EPS)
    return (jax.nn.silu(zg.astype(f32)) * merge_heads(o)).astype(zq.dtype)


def rwkv7_scan(r, w, k, v, kk, a, reverse):
    def step(state, inp):
        rt, wt, kt, vt, kkt, at = inp
        sa = jnp.einsum('bhvk,bhk->bhv', state, -kkt)
        state = (state * wt[:, :, None, :] + sa[..., None] * (kkt * at)[:, :, None, :]
                 + vt[..., None] * kt[:, :, None, :])
        return state, jnp.einsum('bhvk,bhk->bhv', state, rt)

    b, _, h, n = r.shape
    state0 = jnp.zeros((b, h, n, n), jnp.float32)
    xs = (jnp.moveaxis(r, 1, 0), jnp.moveaxis(w, 1, 0), jnp.moveaxis(k, 1, 0),
          jnp.moveaxis(v, 1, 0), jnp.moveaxis(kk, 1, 0), jnp.moveaxis(a, 1, 0))
    _, out = lax.scan(step, state0, xs, reverse=reverse)
    return jnp.moveaxis(out, 0, 1)


def rwkv7_mixer(zw, mu, w0, w2, a0, a2, g2, k_k, k_a, r_k, ln_w, ln_b):
    f32 = jnp.float32
    G = GROUP_WIDTH
    b, s, _ = zw.shape
    z = zw.astype(f32)
    zp = jnp.pad(z, ((0, 0), (1, 1), (0, 0)))
    z = z + mu * (0.5 * (zp[:, :-2] + zp[:, 2:]) - z)
    r, k, v, wl, al, gl = split_cols(z, [G, G, G, 2 * RWKV_W_RANK, 2 * RWKV_A_RANK, RWKV_G_RANK])
    wl = wl.reshape(b, s, 2, RWKV_W_RANK)
    al = al.reshape(b, s, 2, RWKV_A_RANK)
    w_log = -jax.nn.softplus(-(w0 + jnp.einsum('bsdr,drc->bsdc', jnp.tanh(wl), w2))) - 0.5
    decay = jnp.exp(-jnp.exp(w_log))
    a = jax.nn.sigmoid(a0 + jnp.einsum('bsdr,drc->bsdc', al, a2))
    gate = jax.nn.sigmoid(gl) @ g2
    hs = lambda t: t.reshape(b, s, N_HEADS, HEAD_DIM)
    kk = hs(k * k_k)
    kk = kk / jnp.maximum(jnp.sqrt(jnp.sum(kk * kk, axis=-1, keepdims=True)), 1e-12)
    rh, vh = hs(r), hs(v)
    wkv = jnp.zeros_like(rh)
    bonus = jnp.zeros_like(rh)
    for d, rev in ((0, False), (1, True)):
        kd = hs(k * (1.0 + (a[:, :, d] - 1.0) * k_a))
        wkv = wkv + rwkv7_scan(rh, hs(decay[:, :, d]), kd, vh, kk, hs(a[:, :, d]), rev)
        bonus = bonus + jnp.sum(rh * kd * r_k, axis=-1, keepdims=True) * vh
    mean = jnp.mean(wkv, axis=-1, keepdims=True)
    var = jnp.mean((wkv - mean) ** 2, axis=-1, keepdims=True)
    wkv = (wkv - mean) * lax.rsqrt(var + RWKV_GN_EPS)
    out = wkv.reshape(b, s, G) * ln_w + ln_b + bonus.reshape(b, s, G)
    return (out * gate).astype(zw.dtype)


def mlstm_chunkwise(q, k, v, log_i, log_f):
    b, h, _, d = q.shape
    tril = jnp.tril(jnp.ones((CHUNK, CHUNK), dtype=bool))

    def step(carry, inp):
        c_mat, n_vec, m_prev = carry
        qc, kc, vc, li, lf = inp
        bcum = jnp.cumsum(lf, axis=-1)
        d_log = jnp.where(tril, bcum[..., :, None] - bcum[..., None, :] + li[..., None, :], -jnp.inf)
        inter_log = bcum + m_prev[..., None]
        m_t = jnp.maximum(inter_log, jnp.max(d_log, axis=-1))
        inter_w = jnp.exp(inter_log - m_t)
        s = jnp.einsum('bhtd,bhsd->bhts', qc, kc) * jnp.exp(d_log - m_t[..., None])
        num = (jnp.einsum('bhts,bhsv->bhtv', s, vc)
               + inter_w[..., None] * jnp.einsum('bhvk,bhtk->bhtv', c_mat, qc))
        den = jnp.sum(s, axis=-1) + inter_w * jnp.einsum('bhk,bhtk->bht', n_vec, qc)
        out = num / jnp.maximum(jnp.abs(den), jnp.exp(-m_t))[..., None]
        b_end = bcum[..., -1]
        w_log = b_end[..., None] - bcum + li
        m_new = jnp.maximum(b_end + m_prev, jnp.max(w_log, axis=-1))
        old_w = jnp.exp(b_end + m_prev - m_new)
        new_w = jnp.exp(w_log - m_new[..., None])
        c_mat = old_w[..., None, None] * c_mat + jnp.einsum('bhs,bhsv,bhsk->bhvk', new_w, vc, kc)
        n_vec = old_w[..., None] * n_vec + jnp.einsum('bhs,bhsk->bhk', new_w, kc)
        return (c_mat, n_vec, m_new), out

    carry0 = (jnp.zeros((b, h, d, d), jnp.float32), jnp.zeros((b, h, d), jnp.float32),
              jnp.zeros((b, h), jnp.float32))
    _, out = lax.scan(step, carry0, (to_chunks(q), to_chunks(k), to_chunks(v),
                                     to_chunks(log_i), to_chunks(log_f)))
    return from_chunks(out)


def mlstm_mixer(zq, zk, zv, zo, zi, zf, i_bias, f_bias, norm_w):
    f32 = jnp.float32
    b, s, _ = zq.shape
    q = heads(zq.astype(f32)) * HEAD_DIM ** -0.5
    k = heads(zk.astype(f32))
    v = heads(zv.astype(f32))
    gi = zi.astype(f32).reshape(b, s, 2, N_HEADS) + i_bias
    gf = zf.astype(f32).reshape(b, s, 2, N_HEADS) + f_bias
    log_i = jnp.transpose(gi, (2, 0, 3, 1))
    log_f = jnp.transpose(jax.nn.log_sigmoid(gf), (2, 0, 3, 1))
    fwd = mlstm_chunkwise(q, k, v, log_i[0], log_f[0])
    bwd = jnp.flip(mlstm_chunkwise(jnp.flip(q, 2), jnp.flip(k, 2), jnp.flip(v, 2),
                                   jnp.flip(log_i[1], -1), jnp.flip(log_f[1], -1)), 2)
    hsum = fwd + bwd
    hsum = hsum * lax.rsqrt(jnp.mean(hsum * hsum, axis=-1, keepdims=True) + RMS_EPS)
    return (jax.nn.sigmoid(zo.astype(f32)) * merge_heads(hsum) * norm_w).astype(zq.dtype)


def setup_inputs(seed: int = 0) -> dict:
    key = jax.random.key(seed)
    ks = jax.random.split(key, 32)
    f32 = jnp.float32
    G = GROUP_WIDTH
    nrm = lambda k, shape, fan_in: jax.random.normal(k, shape, f32) * fan_in ** -0.5
    noise = lambda k, shape, sc: sc * jax.random.normal(k, shape, f32)
    p = jnp.arange(RET_DECAY_EXP0, RET_DECAY_EXP0 + N_HEADS, dtype=f32)
    return {
        'x': jax.random.normal(ks[0], (BATCH, SEQ, D_MODEL), f32),
        'norm_g': 1.0 + noise(ks[1], (DEPTH, 6, D_MODEL), 0.02),
        'ffn_w_gate': nrm(ks[2], (DEPTH, 2, D_MODEL, D_FF), D_MODEL),
        'ffn_w_up': nrm(ks[3], (DEPTH, 2, D_MODEL, D_FF), D_MODEL),
        'ffn_w_down': nrm(ks[4], (DEPTH, 2, D_FF, D_MODEL), D_FF),
        'w_in': nrm(ks[5], (DEPTH, D_MODEL, P_TOTAL), D_MODEL),
        'w_out': nrm(ks[6], (DEPTH, D_MODEL, D_MODEL), D_MODEL),
        'conv_w': nrm(ks[7], (DEPTH, CONV_WIDTH, G), CONV_WIDTH),
        'conv_b': noise(ks[8], (DEPTH, G), 0.02),
        'ret_decay_logit': jnp.log(2.0 ** p - 1.0) + noise(ks[9], (DEPTH, 2, N_HEADS), 0.05),
        'rwkv_mu': jax.random.uniform(ks[10], (DEPTH, RWKV_COLS), f32),
        'rwkv_w0': jax.random.uniform(ks[11], (DEPTH, 2, G), f32, minval=-6.0, maxval=-1.0),
        'rwkv_w2': nrm(ks[12], (DEPTH, 2, RWKV_W_RANK, G), RWKV_W_RANK),
        'rwkv_a0': noise(ks[13], (DEPTH, 2, G), 0.1),
        'rwkv_a2': nrm(ks[14], (DEPTH, 2, RWKV_A_RANK, G), RWKV_A_RANK),
        'rwkv_g2': nrm(ks[15], (DEPTH, RWKV_G_RANK, G), RWKV_G_RANK),
        'rwkv_k_k': 0.85 + noise(ks[16], (DEPTH, G), 0.02),
        'rwkv_k_a': 1.0 + noise(ks[17], (DEPTH, G), 0.02),
        'rwkv_r_k': noise(ks[18], (DEPTH, N_HEADS, HEAD_DIM), 0.1),
        'rwkv_ln_w': 1.0 + noise(ks[19], (DEPTH, G), 0.02),
        'rwkv_ln_b': noise(ks[20], (DEPTH, G), 0.02),
        'mlstm_i_bias': noise(ks[21], (DEPTH, 2, N_HEADS), 0.1),
        'mlstm_f_bias': jax.random.uniform(ks[22], (DEPTH, 2, N_HEADS), f32, minval=3.0, maxval=6.0),
        'mlstm_norm_w': 1.0 + noise(ks[23], (DEPTH, G), 0.02),
    }


def reference(x, norm_g, ffn_w_gate, ffn_w_up, ffn_w_down, w_in, w_out, conv_w, conv_b,
              ret_decay_logit, rwkv_mu, rwkv_w0, rwkv_w2, rwkv_a0, rwkv_a2, rwkv_g2,
              rwkv_k_k, rwkv_k_a, rwkv_r_k, rwkv_ln_w, rwkv_ln_b,
              mlstm_i_bias, mlstm_f_bias, mlstm_norm_w):
    G = GROUP_WIDTH
    cos, sin = rope_tables(x.shape[1])
    widths = [G, G, G,
              G, G, G, G,
              RWKV_COLS,
              G, G, G, G, 2 * N_HEADS, 2 * N_HEADS]
    for l in range(DEPTH):
        g = norm_g[l]
        h = rms_norm(x, g[0])
        x = x + 0.5 * rms_norm(swiglu(h, ffn_w_gate[l, 0], ffn_w_up[l, 0], ffn_w_down[l, 0]), g[1])
        h = rms_norm(x, g[2])
        z = h @ w_in[l]
        cb, cc, ch, rq, rk, rv, rg, zw, mq, mk, mv, mo, mi, mf = split_cols(z, widths)
        y_conv = short_conv_mixer(cb, cc, ch, conv_w[l], conv_b[l])
        y_ret = retention_mixer(rq, rk, rv, rg, ret_decay_logit[l], cos, sin)
        y_rwkv = rwkv7_mixer(zw, rwkv_mu[l], rwkv_w0[l], rwkv_w2[l], rwkv_a0[l], rwkv_a2[l],
                             rwkv_g2[l], rwkv_k_k[l], rwkv_k_a[l], rwkv_r_k[l],
                             rwkv_ln_w[l], rwkv_ln_b[l])
        y_mlstm = mlstm_mixer(mq, mk, mv, mo, mi, mf, mlstm_i_bias[l], mlstm_f_bias[l], mlstm_norm_w[l])
        y = jnp.concatenate([y_conv.astype(z.dtype), y_ret.astype(z.dtype),
                             y_rwkv.astype(z.dtype), y_mlstm.astype(z.dtype)], axis=-1) @ w_out[l]
        x = x + rms_norm(y, g[3])
        h = rms_norm(x, g[4])
        x = x + 0.5 * rms_norm(swiglu(h, ffn_w_gate[l, 1], ffn_w_up[l, 1], ffn_w_down[l, 1]), g[5])
    return x
```

```python
import functools

import numpy as np
import jax
import jax.numpy as jnp
from jax import lax
from jax.experimental import pallas as pl
from jax.experimental.pallas import tpu as pltpu

F32 = jnp.float32
BF16 = jnp.bfloat16

D_MODEL = 1024
GROUP = 256
N_HEADS = 4
HEAD_DIM = 64
D_FF = 2816
RWKV_COLS = 960
RMS_EPS = 1e-6
RWKV_GN_EPS = 64e-5

LANES_V7X = 128
SUBLANES_V7X = 8
VMEM_LIMIT_BYTES = 56 * 1024 * 1024

ROW_TILE = 512
FFN_CHUNK = 256
RET_CHUNK = 128
RWKV_CHUNK = 64

Z_COLS = 4096
ZB_Q, ZB_K, ZB_V, ZB_G = 0, 1, 2, 3
ZC_BLOCK_1024 = 1
ZA_B, ZA_C, ZA_H = 8, 9, 10
ZD_Q, ZD_K, ZD_V, ZD_O = 11, 12, 13, 14
ZD_GI_128, ZD_GF_128 = 30, 31


def _mm(a, b):
    return jnp.dot(a.astype(BF16), b.astype(BF16), preferred_element_type=F32)


def _mm_nt(a, b):
    return lax.dot_general(a.astype(BF16), b.astype(BF16), (((1,), (1,)), ((), ())),
                           preferred_element_type=F32)


def _mm_tn(a, b):
    return lax.dot_general(a.astype(BF16), b.astype(BF16), (((0,), (0,)), ((), ())),
                           preferred_element_type=F32)


def _mm_f32(a, b):
    return jnp.dot(a, b, preferred_element_type=F32, precision=lax.Precision.HIGHEST)


def _split3(x):
    hi = x.astype(BF16)
    r1 = x - hi.astype(F32)
    mid = r1.astype(BF16)
    lo = (r1 - mid.astype(F32)).astype(BF16)
    return hi, mid, lo


def _sel_mm(sel, x):
    hi, mid, lo = _split3(x)
    s = sel.astype(BF16)
    return (jnp.dot(s, hi, preferred_element_type=F32) + jnp.dot(s, mid, preferred_element_type=F32)
            + jnp.dot(s, lo, preferred_element_type=F32))


def _mm_sel(x, sel):
    hi, mid, lo = _split3(x)
    s = sel.astype(BF16)
    return (jnp.dot(hi, s, preferred_element_type=F32) + jnp.dot(mid, s, preferred_element_type=F32)
            + jnp.dot(lo, s, preferred_element_type=F32))


def _rms_norm(x, g):
    return x * lax.rsqrt(jnp.mean(x * x, axis=-1, keepdims=True) + RMS_EPS) * g


def _softplus(x):
    return jnp.maximum(x, 0.0) + jnp.log(1.0 + jnp.exp(-jnp.abs(x)))


def _sigmoid(x):
    return 1.0 / (1.0 + jnp.exp(-x))


def _head_mask(h):
    lane = lax.broadcasted_iota(jnp.int32, (1, GROUP), 1)
    return ((lane >= h * HEAD_DIM) & (lane < (h + 1) * HEAD_DIM)).astype(F32)


def _order_masks(n, reverse):
    row = lax.broadcasted_iota(jnp.int32, (n, n), 0)
    col = lax.broadcasted_iota(jnp.int32, (n, n), 1)
    if reverse:
        return col >= row, col > row
    return col <= row, col < row


def _params(*sem):
    return pltpu.CompilerParams(dimension_semantics=sem, vmem_limit_bytes=VMEM_LIMIT_BYTES)


def _ffn_body(x_ref, gin_ref, gout_ref, wg_ref, wu_ref, wd_ref, o_ref, acc_ref):
    x = x_ref[...]
    h = _rms_norm(x, gin_ref[...]).astype(BF16)
    for c in range(D_FF // FFN_CHUNK):
        lo = c * FFN_CHUNK
        gate = jnp.dot(h, wg_ref[:, lo:lo + FFN_CHUNK], preferred_element_type=F32)
        up = jnp.dot(h, wu_ref[:, lo:lo + FFN_CHUNK], preferred_element_type=F32)
        act = (gate * _sigmoid(gate) * up).astype(BF16)
        part = jnp.dot(act, wd_ref[lo:lo + FFN_CHUNK, :], preferred_element_type=F32)
        if c == 0:
            acc_ref[...] = part
        else:
            acc_ref[...] += part
    o_ref[...] = x + 0.5 * _rms_norm(acc_ref[...], gout_ref[...])


def _ffn(x, g_in, g_out, w_gate, w_up, w_down):
    n = x.shape[0]
    tm = min(ROW_TILE, n)
    const = lambda i: (0, 0)
    return pl.pallas_call(
        _ffn_body,
        grid=(n // tm,),
        in_specs=[
            pl.BlockSpec((tm, D_MODEL), lambda i: (i, 0)),
            pl.BlockSpec((1, D_MODEL), const),
            pl.BlockSpec((1, D_MODEL), const),
            pl.BlockSpec((D_MODEL, D_FF), const),
            pl.BlockSpec((D_MODEL, D_FF), const),
            pl.BlockSpec((D_FF, D_MODEL), const),
        ],
        out_specs=pl.BlockSpec((tm, D_MODEL), lambda i: (i, 0)),
        out_shape=jax.ShapeDtypeStruct((n, D_MODEL), F32),
        scratch_shapes=[pltpu.VMEM((tm, D_MODEL), F32)],
        compiler_params=_params("parallel"),
        name="ffn",
    )(x, g_in, g_out, w_gate, w_up, w_down)


def _inproj_body(x_ref, g_ref, w_ref, z_ref):
    h = _rms_norm(x_ref[...], g_ref[...]).astype(BF16)
    z_ref[...] = jnp.dot(h, w_ref[...], preferred_element_type=F32)


def _inproj(x, g, w):
    n = x.shape[0]
    tm = min(ROW_TILE, n)
    return pl.pallas_call(
        _inproj_body,
        grid=(n // tm,),
        in_specs=[
            pl.BlockSpec((tm, D_MODEL), lambda i: (i, 0)),
            pl.BlockSpec((1, D_MODEL), lambda i: (0, 0)),
            pl.BlockSpec((D_MODEL, Z_COLS), lambda i: (0, 0)),
        ],
        out_specs=pl.BlockSpec((tm, Z_COLS), lambda i: (i, 0)),
        out_shape=jax.ShapeDtypeStruct((n, Z_COLS), F32),
        compiler_params=_params("parallel"),
        name="inproj",
    )(x, g, w)


def _chunk_rows(nc, reverse, col):
    if reverse:
        return lambda b, c: (b * nc + (nc - 1 - c), col)
    return lambda b, c: (b * nc + c, col)


def _const2(b, c):
    return (0, 0)


def _ret_body(*refs, reverse):
    if reverse:
        (lgs_ref, q_ref, k_ref, v_ref, cos_ref, sin_ref, lgv_ref, bd_ref, g_ref, fwd_ref,
         o_ref, st_ref) = refs
    else:
        lgs_ref, q_ref, k_ref, v_ref, cos_ref, sin_ref, lgv_ref, bd_ref, o_ref, st_ref = refs
    n = RET_CHUNK

    @pl.when(pl.program_id(1) == 0)
    def _():
        st_ref[...] = jnp.zeros_like(st_ref)

    lane = lax.broadcasted_iota(jnp.int32, (1, GROUP), 1)
    first_half = (lane % HEAD_DIM) < (HEAD_DIM // 2)
    cos = cos_ref[...]
    sin = sin_ref[...]

    def rot(t):
        swapped = jnp.where(first_half, pltpu.roll(t, GROUP - HEAD_DIM // 2, 1),
                            pltpu.roll(t, HEAD_DIM // 2, 1))
        return t * cos + swapped * sin

    q = rot(q_ref[...]) * HEAD_DIM ** -0.5
    k = rot(k_ref[...])
    v = v_ref[...]
    lg = lgv_ref[...]
    ti = lax.broadcasted_iota(jnp.int32, (n, 1), 0).astype(F32)
    if reverse:
        q_exp, k_exp = n - ti, ti
    else:
        q_exp, k_exp = ti + 1.0, n - 1.0 - ti
    st = st_ref[...]
    out = jnp.exp(q_exp * lg) * _mm(q, st)
    row = lax.broadcasted_iota(jnp.int32, (n, n), 0)
    col = lax.broadcasted_iota(jnp.int32, (n, n), 1)
    if reverse:
        rel, msk = (col - row).astype(F32), col > row
    else:
        rel, msk = (row - col).astype(F32), col <= row
    kb = k.astype(BF16)
    for h in range(N_HEADS):
        mh = _head_mask(h)
        intra = jnp.where(msk, jnp.exp(jnp.where(msk, rel, 0.0) * lgs_ref[h]), 0.0)
        s = _mm_nt(q * mh, kb) * intra
        out = out + _mm(s, v * mh)
    st_ref[...] = jnp.exp(float(n) * lg) * st + bd_ref[...] * _mm_tn(k * jnp.exp(k_exp * lg), v)
    if reverse:
        tot = fwd_ref[...] + out
        ms = _mm_sel(tot * tot, bd_ref[...]) * (1.0 / HEAD_DIM)
        g = g_ref[...]
        o_ref[...] = g * _sigmoid(g) * (tot * lax.rsqrt(ms + RMS_EPS))
    else:
        o_ref[...] = out


def _retention(z, batch, seq, cos_t, sin_t, log_gamma, bd):
    n = RET_CHUNK
    nc = seq // n
    blk = lambda col, rev: pl.BlockSpec((n, GROUP), _chunk_rows(nc, rev, col))

    def tab(rev):
        if rev:
            return pl.BlockSpec((n, GROUP), lambda b, c: (nc - 1 - c, 0))
        return pl.BlockSpec((n, GROUP), lambda b, c: (c, 0))

    out_shape = jax.ShapeDtypeStruct((batch * seq, GROUP), F32)
    smem = pl.BlockSpec(memory_space=pltpu.SMEM)
    lgv = jnp.repeat(log_gamma, HEAD_DIM, axis=1)
    common = dict(grid=(batch, nc), out_shape=out_shape,
                  scratch_shapes=[pltpu.VMEM((GROUP, GROUP), F32)],
                  compiler_params=_params("parallel", "arbitrary"))
    fwd = pl.pallas_call(
        functools.partial(_ret_body, reverse=False),
        in_specs=[smem, blk(ZB_Q, False), blk(ZB_K, False), blk(ZB_V, False), tab(False), tab(False),
                  pl.BlockSpec((1, GROUP), _const2), pl.BlockSpec((GROUP, GROUP), _const2)],
        out_specs=blk(0, False), name="ret_fwd", **common,
    )(log_gamma[0], z, z, z, cos_t, sin_t, lgv[0:1], bd)
    return pl.pallas_call(
        functools.partial(_ret_body, reverse=True),
        in_specs=[smem, blk(ZB_Q, True), blk(ZB_K, True), blk(ZB_V, True), tab(True), tab(True),
                  pl.BlockSpec((1, GROUP), _const2), pl.BlockSpec((GROUP, GROUP), _const2),
                  blk(ZB_G, True), blk(0, True)],
        out_specs=blk(0, True), name="ret_bwd", **common,
    )(log_gamma[1], z, z, z, cos_t, sin_t, lgv[1:2], bd, z, fwd)


def _mlstm_body(*refs, reverse, direction):
    if reverse:
        (q_ref, k_ref, v_ref, gi_ref, gf_ref, ib_ref, fb_ref, ed_ref, bd_ref, o_gate_ref, fwd_ref,
         nw_ref, o_ref, c_ref, n_ref, m_ref) = refs
    else:
        (q_ref, k_ref, v_ref, gi_ref, gf_ref, ib_ref, fb_ref, ed_ref, bd_ref,
         o_ref, c_ref, n_ref, m_ref) = refs
    n = RET_CHUNK

    @pl.when(pl.program_id(1) == 0)
    def _():
        c_ref[...] = jnp.zeros_like(c_ref)
        n_ref[...] = jnp.zeros_like(n_ref)
        m_ref[...] = jnp.zeros_like(m_ref)

    q = q_ref[...] * HEAD_DIM ** -0.5
    k = k_ref[...]
    v = v_ref[...]
    gi = gi_ref[...] + ib_ref[...]
    lf = -_softplus(-(gf_ref[...] + fb_ref[...]))
    incl, _ = _order_masks(n, reverse)
    bcum = _sel_mm(incl, lf)
    row_terms = (gi - bcum).T
    ed = ed_ref[...]
    li_x = _mm_sel(gi, ed)
    bcum_x = _mm_sel(bcum, ed)
    last = 0 if reverse else n - 1
    b_end = bcum_x[last:last + 1, :]
    m_x = m_ref[...]
    n_x = n_ref[...]
    cmat = c_ref[...]
    num = jnp.zeros((n, GROUP), F32)
    den = jnp.zeros((n, GROUP), F32)
    mt_x = jnp.zeros((n, GROUP), F32)
    kb = k.astype(BF16)
    for h in range(N_HEADS):
        j = direction * N_HEADS + h
        mh = _head_mask(h)
        colv = bcum[:, j:j + 1]
        rowv = row_terms[j:j + 1, :]
        m_prev = m_x[:, h * HEAD_DIM:h * HEAD_DIM + 1]
        d_log = jnp.where(incl, colv + rowv, -jnp.inf)
        mt = jnp.maximum(colv + m_prev, jnp.max(d_log, axis=1, keepdims=True))
        s = _mm_nt(q * mh, kb) * jnp.exp(d_log - mt)
        num = num + _mm(s, v * mh)
        den = den + jnp.sum(s, axis=1, keepdims=True) * mh
        mt_x = mt_x + mt * mh
    inter_w = jnp.exp(bcum_x + m_x - mt_x)
    num = num + inter_w * _mm(q, cmat)
    den = den + inter_w * _mm_sel(q * n_x, bd_ref[...])
    out = num / jnp.maximum(jnp.abs(den), jnp.exp(-mt_x))
    w_log = b_end - bcum_x + li_x
    m_new = jnp.maximum(b_end + m_x, jnp.max(w_log, axis=0, keepdims=True))
    old_w = jnp.exp(b_end + m_x - m_new)
    new_w = jnp.exp(w_log - m_new)
    c_ref[...] = old_w * cmat + bd_ref[...] * _mm_tn(k * new_w, v)
    n_ref[...] = old_w * n_x + jnp.sum(new_w * k, axis=0, keepdims=True)
    m_ref[...] = m_new
    if reverse:
        tot = fwd_ref[...] + out
        ms = _mm_sel(tot * tot, bd_ref[...]) * (1.0 / HEAD_DIM)
        o_ref[...] = _sigmoid(o_gate_ref[...]) * (tot * lax.rsqrt(ms + RMS_EPS)) * nw_ref[...]
    else:
        o_ref[...] = out


def _mlstm(z, batch, seq, i_bias, f_bias, norm_w, expand, bd):
    n = RET_CHUNK
    nc = seq // n
    blk = lambda col, rev: pl.BlockSpec((n, GROUP), _chunk_rows(nc, rev, col))
    gblk = lambda col, rev: pl.BlockSpec((n, LANES_V7X), _chunk_rows(nc, rev, col))
    out_shape = jax.ShapeDtypeStruct((batch * seq, GROUP), F32)
    common = dict(grid=(batch, nc), out_shape=out_shape,
                  scratch_shapes=[pltpu.VMEM((GROUP, GROUP), F32), pltpu.VMEM((1, GROUP), F32),
                                  pltpu.VMEM((1, GROUP), F32)],
                  compiler_params=_params("parallel", "arbitrary"))

    def specs(rev):
        return [blk(ZD_Q, rev), blk(ZD_K, rev), blk(ZD_V, rev), gblk(ZD_GI_128, rev), gblk(ZD_GF_128, rev),
                pl.BlockSpec((1, LANES_V7X), _const2), pl.BlockSpec((1, LANES_V7X), _const2),
                pl.BlockSpec((LANES_V7X, GROUP), _const2), pl.BlockSpec((GROUP, GROUP), _const2)]

    fwd = pl.pallas_call(
        functools.partial(_mlstm_body, reverse=False, direction=0),
        in_specs=specs(False), out_specs=blk(0, False), name="mlstm_fwd", **common,
    )(z, z, z, z, z, i_bias, f_bias, expand[0], bd)
    return pl.pallas_call(
        functools.partial(_mlstm_body, reverse=True, direction=1),
        in_specs=specs(True) + [blk(ZD_O, True), blk(0, True), pl.BlockSpec((1, GROUP), _const2)],
        out_specs=blk(0, True), name="mlstm_bwd", **common,
    )(z, z, z, z, z, i_bias, f_bias, expand[1], bd, z, fwd, norm_w)


def _tri_inverse(lmat, n):
    row = lax.broadcasted_iota(jnp.int32, (n, n), 0)
    col = lax.broadcasted_iota(jnp.int32, (n, n), 1)
    t = jnp.where(row == col, 1.0, 0.0) + lmat
    p = lmat
    span = 1
    while 2 * span < n:
        p = _mm_f32(p, p)
        t = t + _mm_f32(t, p)
        span *= 2
    return t


def _rwkv_body(*refs, reverse, nc):
    if reverse:
        (z_ref, zp_ref, zn_ref, mu_ref, w0_ref, w2_ref, a0_ref, a2_ref, kk_ref, ka_ref, rk_ref, bd_ref,
         g2_ref, lnw_ref, lnb_ref, wkvf_ref, bonf_ref, o_ref, s_ref) = refs
    else:
        (z_ref, zp_ref, zn_ref, mu_ref, w0_ref, w2_ref, a0_ref, a2_ref, kk_ref, ka_ref, rk_ref, bd_ref,
         wkv_ref, bon_ref, s_ref) = refs
    n = RWKV_CHUNK
    step = pl.program_id(1)
    cidx = (nc - 1 - step) if reverse else step

    @pl.when(step == 0)
    def _():
        s_ref[...] = jnp.zeros_like(s_ref)

    z = z_ref[...]
    z_prev = jnp.where(cidx > 0, zp_ref[SUBLANES_V7X - 1:SUBLANES_V7X, :], 0.0)
    z_next = jnp.where(cidx < nc - 1, zn_ref[0:1, :], 0.0)
    ridx = lax.broadcasted_iota(jnp.int32, (n, 1), 0)
    z_m1 = jnp.where(ridx == 0, z_prev, pltpu.roll(z, 1, 0))
    z_p1 = jnp.where(ridx == n - 1, z_next, pltpu.roll(z, n - 1, 0))
    zs = z + mu_ref[...] * (0.5 * (z_m1 + z_p1) - z)
    r = zs[:, 0:GROUP]
    k = zs[:, GROUP:2 * GROUP]
    v = zs[:, 2 * GROUP:3 * GROUP]
    lat1 = zs[:, 3 * GROUP:3 * GROUP + LANES_V7X]
    lat2 = zs[:, 3 * GROUP + LANES_V7X:4 * GROUP]

    w_pre = w0_ref[...] + _mm(jnp.tanh(lat1), w2_ref[...])
    log_decay = -jnp.exp(-_softplus(-w_pre) - 0.5)
    a = _sigmoid(a0_ref[...] + _mm(lat1, a2_ref[...]))
    kk_raw = k * kk_ref[...]
    kk = kk_raw / jnp.maximum(jnp.sqrt(_mm_sel(kk_raw * kk_raw, bd_ref[...])), 1e-12)
    kd = k * (1.0 + (a - 1.0) * ka_ref[...])
    bonus = _mm_sel(r * kd * rk_ref[...], bd_ref[...]) * v

    incl, strict = _order_masks(n, reverse)
    cum_incl = _sel_mm(incl, log_decay)
    cum_excl = cum_incl - log_decay
    last = 0 if reverse else n - 1
    gam_end = jnp.exp(cum_incl[last:last + 1, :])
    a_t = -kk * jnp.exp(cum_excl)
    r_t = r * jnp.exp(cum_incl)
    inv_g = jnp.exp(-cum_incl)
    b_t = (kk * a * inv_g).astype(BF16)
    k_t = (kd * inv_g).astype(BF16)
    state = s_ref[...]
    ar = jnp.concatenate([a_t, r_t], axis=0)
    xo = _mm_nt(ar, state)
    x = xo[0:n]
    o = xo[n:2 * n]
    tinv, m_rb, m_rk = [], [], []
    for h in range(N_HEADS):
        mh = _head_mask(h)
        ar_h = ar * mh
        sb = _mm_nt(ar_h, b_t)
        sk = _mm_nt(ar_h, k_t)
        x = x + _mm(jnp.where(strict, sk[0:n], 0.0), v * mh)
        tinv.append(_tri_inverse(jnp.where(strict, sb[0:n], 0.0), n))
        m_rb.append(jnp.where(incl, sb[n:2 * n], 0.0))
        m_rk.append(jnp.where(incl, sk[n:2 * n], 0.0))
    u = jnp.zeros((n, GROUP), F32)
    for h in range(N_HEADS):
        u = u + _mm_f32(tinv[h], x * _head_mask(h))
    for h in range(N_HEADS):
        mh = _head_mask(h)
        o = o + _mm(m_rb[h], u * mh) + _mm(m_rk[h], v * mh)
    s_ref[...] = gam_end * (state + bd_ref[...] * (_mm_tn(u, b_t) + _mm_tn(v, k_t)))

    if reverse:
        wkv = wkvf_ref[...] + o
        bon = bonf_ref[...] + bonus
        mean = _mm_sel(wkv, bd_ref[...]) * (1.0 / HEAD_DIM)
        cen = wkv - mean
        var = _mm_sel(cen * cen, bd_ref[...]) * (1.0 / HEAD_DIM)
        out = cen * lax.rsqrt(var + RWKV_GN_EPS) * lnw_ref[...] + lnb_ref[...] + bon
        o_ref[...] = out * _mm(_sigmoid(lat2), g2_ref[...])
    else:
        wkv_ref[...] = o
        bon_ref[...] = bonus


def _rwkv(z, batch, seq, p, bd):
    n = RWKV_CHUNK
    nc = seq // n
    rows8 = n // SUBLANES_V7X
    last8 = batch * seq // SUBLANES_V7X - 1

    def cidx(c, rev):
        return (nc - 1 - c) if rev else c

    def zspec(rev):
        return pl.BlockSpec((n, 4 * GROUP), lambda b, c: (b * nc + cidx(c, rev), ZC_BLOCK_1024))

    def prev_spec(rev):
        return pl.BlockSpec((SUBLANES_V7X, 4 * GROUP), lambda b, c: (
            jnp.maximum((b * nc + cidx(c, rev)) * rows8 - 1, 0), ZC_BLOCK_1024))

    def next_spec(rev):
        return pl.BlockSpec((SUBLANES_V7X, 4 * GROUP), lambda b, c: (
            jnp.minimum((b * nc + cidx(c, rev) + 1) * rows8, last8), ZC_BLOCK_1024))

    vec = pl.BlockSpec((1, GROUP), _const2)
    lora = pl.BlockSpec((LANES_V7X, GROUP), _const2)
    out_blk = lambda rev: pl.BlockSpec((n, GROUP), _chunk_rows(nc, rev, 0))
    out_shape = jax.ShapeDtypeStruct((batch * seq, GROUP), F32)

    def specs(rev):
        return [zspec(rev), prev_spec(rev), next_spec(rev), pl.BlockSpec((1, 4 * GROUP), _const2),
                vec, lora, vec, lora, vec, vec, vec, pl.BlockSpec((GROUP, GROUP), _const2)]

    common = dict(grid=(batch, nc), scratch_shapes=[pltpu.VMEM((GROUP, GROUP), F32)],
                  compiler_params=_params("parallel", "arbitrary"))
    wkv_f, bon_f = pl.pallas_call(
        functools.partial(_rwkv_body, reverse=False, nc=nc),
        in_specs=specs(False), out_specs=[out_blk(False), out_blk(False)],
        out_shape=[out_shape, out_shape], name="rwkv_fwd", **common,
    )(z, z, z, p["mu"], p["w0"][0], p["w2"][0], p["a0"][0], p["a2"][0], p["k_k"], p["k_a"], p["r_k"], bd)
    return pl.pallas_call(
        functools.partial(_rwkv_body, reverse=True, nc=nc),
        in_specs=specs(True) + [lora, vec, vec, out_blk(True), out_blk(True)],
        out_specs=out_blk(True), out_shape=out_shape, name="rwkv_bwd", **common,
    )(z, z, z, p["mu"], p["w0"][1], p["w2"][1], p["a0"][1], p["a2"][1], p["k_k"], p["k_a"], p["r_k"], bd,
      p["g2"], p["ln_w"], p["ln_b"], wkv_f, bon_f)


def _outproj_body(x_ref, cb_ref, cc_ref, ch_ref, ccp_ref, chp_ref, ccn_ref, chn_ref, yb_ref, yc_ref, yd_ref,
                  cw_ref, cbias_ref, w_ref, g_ref, o_ref, *, tiles_per_seq):
    tm = x_ref.shape[0]
    t = pl.program_id(0) % tiles_per_seq
    u = cc_ref[...] * ch_ref[...]
    last8 = SUBLANES_V7X - 1
    u_prev = jnp.where(t > 0, ccp_ref[last8:last8 + 1, :] * chp_ref[last8:last8 + 1, :], 0.0)
    u_next = jnp.where(t < tiles_per_seq - 1, ccn_ref[0:1, :] * chn_ref[0:1, :], 0.0)
    ridx = lax.broadcasted_iota(jnp.int32, (tm, 1), 0)
    u_m1 = jnp.where(ridx == 0, u_prev, pltpu.roll(u, 1, 0))
    u_p1 = jnp.where(ridx == tm - 1, u_next, pltpu.roll(u, tm - 1, 0))
    cw = cw_ref[...]
    conv = cw[0:1, :] * u_m1 + cw[1:2, :] * u + cw[2:3, :] * u_p1 + cbias_ref[...]
    y_conv = cb_ref[...] * conv
    y = (_mm(y_conv, w_ref[0:GROUP, :]) + _mm(yb_ref[...], w_ref[GROUP:2 * GROUP, :])
         + _mm(yc_ref[...], w_ref[2 * GROUP:3 * GROUP, :]) + _mm(yd_ref[...], w_ref[3 * GROUP:4 * GROUP, :]))
    o_ref[...] = x_ref[...] + _rms_norm(y, g_ref[...])


def _outproj(x, z, y_ret, y_rwkv, y_mlstm, conv_w8, conv_b, w_out, g, seq):
    n = x.shape[0]
    tm = min(ROW_TILE, seq)
    tiles_per_seq = seq // tm
    rows8 = tm // SUBLANES_V7X
    last8 = n // SUBLANES_V7X - 1
    zblk = lambda col: pl.BlockSpec((tm, GROUP), lambda i: (i, col))
    prev = lambda col: pl.BlockSpec((SUBLANES_V7X, GROUP), lambda i: (jnp.maximum(i * rows8 - 1, 0), col))
    nxt = lambda col: pl.BlockSpec((SUBLANES_V7X, GROUP), lambda i: (jnp.minimum((i + 1) * rows8, last8), col))
    yblk = pl.BlockSpec((tm, GROUP), lambda i: (i, 0))
    const = lambda i: (0, 0)
    return pl.pallas_call(
        functools.partial(_outproj_body, tiles_per_seq=tiles_per_seq),
        grid=(n // tm,),
        in_specs=[pl.BlockSpec((tm, D_MODEL), lambda i: (i, 0)),
                  zblk(ZA_B), zblk(ZA_C), zblk(ZA_H), prev(ZA_C), prev(ZA_H), nxt(ZA_C), nxt(ZA_H),
                  yblk, yblk, yblk,
                  pl.BlockSpec((SUBLANES_V7X, GROUP), const), pl.BlockSpec((1, GROUP), const),
                  pl.BlockSpec((D_MODEL, D_MODEL), const), pl.BlockSpec((1, D_MODEL), const)],
        out_specs=pl.BlockSpec((tm, D_MODEL), lambda i: (i, 0)),
        out_shape=jax.ShapeDtypeStruct((n, D_MODEL), F32),
        compiler_params=_params("parallel"),
        name="outproj",
    )(x, z, z, z, z, z, z, z, y_ret, y_rwkv, y_mlstm, conv_w8, conv_b, w_out, g)


def _relayout_w_in(w):
    pad = lambda c: jnp.zeros((D_MODEL, c), w.dtype)
    a, b = w[:, 0:768], w[:, 768:1792]
    c = w[:, 1792:1792 + RWKV_COLS]
    d0 = 1792 + RWKV_COLS
    d = w[:, d0:d0 + 4 * GROUP]
    gi = w[:, d0 + 4 * GROUP:d0 + 4 * GROUP + 2 * N_HEADS]
    gf = w[:, d0 + 4 * GROUP + 2 * N_HEADS:d0 + 4 * GROUP + 4 * N_HEADS]
    gpad = LANES_V7X - 2 * N_HEADS
    return jnp.concatenate([b, c, pad(4 * GROUP - RWKV_COLS), a, d, gi, pad(gpad), gf, pad(gpad)],
                           axis=1).astype(BF16)


def _block_diag_ones():
    idx = np.arange(GROUP) // HEAD_DIM
    return jnp.asarray((idx[:, None] == idx[None, :]).astype(np.float32))


def _gate_expand():
    j = np.arange(LANES_V7X)[:, None]
    head = np.arange(GROUP)[None, :] // HEAD_DIM
    return jnp.asarray(np.stack([(j == d * N_HEADS + head) for d in range(2)]).astype(np.float32))


def _rope_tables(seq):
    inv = 10000.0 ** (-jnp.arange(0, HEAD_DIM, 2, dtype=F32) / HEAD_DIM)
    ang = jnp.arange(seq, dtype=F32)[:, None] * inv[None, :]
    cos, sin = jnp.cos(ang), jnp.sin(ang)
    cos_t = jnp.tile(jnp.concatenate([cos, cos], axis=1), (1, N_HEADS))
    sin_t = jnp.tile(jnp.concatenate([-sin, sin], axis=1), (1, N_HEADS))
    return cos_t, sin_t


def _lane_pad(v, width):
    v = v.reshape(1, -1)
    return jnp.pad(v, ((0, 0), (0, width - v.shape[1])))


def _rwkv_params(mu, w0, w2, a0, a2, g2, k_k, k_a, r_k, ln_w, ln_b):
    rank_w, rank_a, rank_g = w2.shape[1], a2.shape[1], g2.shape[0]
    w2p = jnp.zeros((2, LANES_V7X, GROUP), F32)
    a2p = jnp.zeros((2, LANES_V7X, GROUP), F32)
    for d in range(2):
        w2p = w2p.at[d, d * rank_w:(d + 1) * rank_w].set(w2[d])
        a2p = a2p.at[d, 2 * rank_w + d * rank_a:2 * rank_w + (d + 1) * rank_a].set(a2[d])
    g2p = jnp.zeros((LANES_V7X, GROUP), F32).at[0:rank_g].set(g2)
    return dict(mu=_lane_pad(mu, 4 * GROUP), w0=w0.reshape(2, 1, GROUP), w2=w2p.astype(BF16),
                a0=a0.reshape(2, 1, GROUP), a2=a2p.astype(BF16), g2=g2p.astype(BF16),
                k_k=k_k.reshape(1, GROUP), k_a=k_a.reshape(1, GROUP), r_k=r_k.reshape(1, GROUP),
                ln_w=ln_w.reshape(1, GROUP), ln_b=ln_b.reshape(1, GROUP))


def kernel(x, norm_g, ffn_w_gate, ffn_w_up, ffn_w_down, w_in, w_out, conv_w, conv_b, ret_decay_logit, rwkv_mu, rwkv_w0, rwkv_w2, rwkv_a0, rwkv_a2, rwkv_g2, rwkv_k_k, rwkv_k_a, rwkv_r_k, rwkv_ln_w, rwkv_ln_b, mlstm_i_bias, mlstm_f_bias, mlstm_norm_w):
    batch, seq, _ = x.shape
    depth = norm_g.shape[0]
    assert seq % ROW_TILE == 0 and seq % RET_CHUNK == 0 and seq % RWKV_CHUNK == 0
    cos_t, sin_t = _rope_tables(seq)
    bd = _block_diag_ones()
    expand = _gate_expand()
    xf = x.reshape(batch * seq, D_MODEL)
    for l in range(depth):
        g = norm_g[l].reshape(6, 1, D_MODEL)
        xf = _ffn(xf, g[0], g[1], ffn_w_gate[l, 0].astype(BF16), ffn_w_up[l, 0].astype(BF16),
                  ffn_w_down[l, 0].astype(BF16))
        z = _inproj(xf, g[2], _relayout_w_in(w_in[l]))
        log_gamma = jax.nn.log_sigmoid(ret_decay_logit[l].astype(F32))
        y_ret = _retention(z, batch, seq, cos_t, sin_t, log_gamma, bd)
        y_rwkv = _rwkv(z, batch, seq, _rwkv_params(
            rwkv_mu[l], rwkv_w0[l], rwkv_w2[l], rwkv_a0[l], rwkv_a2[l], rwkv_g2[l], rwkv_k_k[l],
            rwkv_k_a[l], rwkv_r_k[l], rwkv_ln_w[l], rwkv_ln_b[l]), bd)
        y_mlstm = _mlstm(z, batch, seq, _lane_pad(mlstm_i_bias[l], LANES_V7X),
                         _lane_pad(mlstm_f_bias[l], LANES_V7X), mlstm_norm_w[l].reshape(1, GROUP), expand, bd)
        conv_w8 = jnp.pad(conv_w[l], ((0, SUBLANES_V7X - conv_w.shape[1]), (0, 0)))
        xf = _outproj(xf, z, y_ret, y_rwkv, y_mlstm, conv_w8, conv_b[l].reshape(1, GROUP),
                      w_out[l].astype(BF16), g[3], seq)
        xf = _ffn(xf, g[4], g[5], ffn_w_gate[l, 1].astype(BF16), ffn_w_up[l, 1].astype(BF16),
                  ffn_w_down[l, 1].astype(BF16))
    return xf.reshape(batch, seq, D_MODEL)
```

```python
import functools

import numpy as np
import jax
import jax.numpy as jnp
from jax import lax
from jax.experimental import pallas as pl
from jax.experimental.pallas import tpu as pltpu

F32 = jnp.float32
BF16 = jnp.bfloat16

D_MODEL = 1024
GROUP = 256
N_HEADS = 4
HEAD_DIM = 64
D_FF = 2816
RWKV_COLS = 960
RMS_EPS = 1e-6
RWKV_GN_EPS = 64e-5

LANES_V7X = 128
SUBLANES_V7X = 8
VMEM_LIMIT_BYTES = 56 * 1024 * 1024

ROW_TILE = 512
FFN_CHUNK = 256
RET_CHUNK = 128
RWKV_CHUNK = 64

HALF = LANES_V7X
N_HALVES = GROUP // HALF
HEADS_PER_HALF = HALF // HEAD_DIM

Z_COLS = 4096
ZB_Q, ZB_K, ZB_V, ZB_G = 0, 1, 2, 3
ZC_BLOCK_1024 = 1
ZA_B, ZA_C, ZA_H = 8, 9, 10
ZD_Q, ZD_K, ZD_V, ZD_O = 11, 12, 13, 14
ZD_GI_128, ZD_GF_128 = 30, 31


def _mm(a, b):
    return jnp.dot(a.astype(BF16), b.astype(BF16), preferred_element_type=F32)


def _mm_nt(a, b):
    return lax.dot_general(a.astype(BF16), b.astype(BF16), (((1,), (1,)), ((), ())),
                           preferred_element_type=F32)


def _mm_tn(a, b):
    return lax.dot_general(a.astype(BF16), b.astype(BF16), (((0,), (0,)), ((), ())),
                           preferred_element_type=F32)


def _split3(x):
    hi = x.astype(BF16)
    r1 = x - hi.astype(F32)
    mid = r1.astype(BF16)
    lo = (r1 - mid.astype(F32)).astype(BF16)
    return hi, mid, lo


def _sel_mm(sel, x):
    hi, mid, lo = _split3(x)
    s = sel.astype(BF16)
    return (jnp.dot(s, hi, preferred_element_type=F32) + jnp.dot(s, mid, preferred_element_type=F32)
            + jnp.dot(s, lo, preferred_element_type=F32))


def _mm_sel(x, sel):
    hi, mid, lo = _split3(x)
    s = sel.astype(BF16)
    return (jnp.dot(hi, s, preferred_element_type=F32) + jnp.dot(mid, s, preferred_element_type=F32)
            + jnp.dot(lo, s, preferred_element_type=F32))


def _rms_norm(x, g):
    return x * lax.rsqrt(jnp.mean(x * x, axis=-1, keepdims=True) + RMS_EPS) * g


def _softplus(x):
    return jnp.maximum(x, 0.0) + jnp.log(1.0 + jnp.exp(-jnp.abs(x)))


def _sigmoid(x):
    return 1.0 / (1.0 + jnp.exp(-x))


def _order_masks(n, reverse):
    row = lax.broadcasted_iota(jnp.int32, (n, n), 0)
    col = lax.broadcasted_iota(jnp.int32, (n, n), 1)
    if reverse:
        return col >= row, col > row
    return col <= row, col < row


def _sub_head_masks(dtype):
    lane = lax.broadcasted_iota(jnp.int32, (1, HALF), 1)
    return [((lane >= j * HEAD_DIM) & (lane < (j + 1) * HEAD_DIM)).astype(dtype)
            for j in range(HEADS_PER_HALF)]


def _lanes_of(half):
    return slice(half * HALF, (half + 1) * HALF)


def _units_and_chains(nb):
    units = [(b, half) for b in range(nb) for half in range(N_HALVES)]
    chains = [(b, half, j) for (b, half) in units for j in range(HEADS_PER_HALF)]
    return units, chains


def _join_halves(per_unit, b):
    return jnp.concatenate([per_unit[(b, half)] for half in range(N_HALVES)], axis=1)


def _params(*sem):
    return pltpu.CompilerParams(dimension_semantics=sem, vmem_limit_bytes=VMEM_LIMIT_BYTES)


def _ffn_body(x_ref, gin_ref, gout_ref, wg_ref, wu_ref, wd_ref, o_ref, acc_ref):
    x = x_ref[...]
    h = _rms_norm(x, gin_ref[...]).astype(BF16)
    for c in range(D_FF // FFN_CHUNK):
        lo = c * FFN_CHUNK
        gate = jnp.dot(h, wg_ref[:, lo:lo + FFN_CHUNK], preferred_element_type=F32)
        up = jnp.dot(h, wu_ref[:, lo:lo + FFN_CHUNK], preferred_element_type=F32)
        act = (gate * _sigmoid(gate) * up).astype(BF16)
        part = jnp.dot(act, wd_ref[lo:lo + FFN_CHUNK, :], preferred_element_type=F32)
        if c == 0:
            acc_ref[...] = part
        else:
            acc_ref[...] += part
    o_ref[...] = x + 0.5 * _rms_norm(acc_ref[...], gout_ref[...])


def _ffn(x, g_in, g_out, w_gate, w_up, w_down):
    n = x.shape[0]
    tm = min(ROW_TILE, n)
    const = lambda i: (0, 0)
    return pl.pallas_call(
        _ffn_body,
        grid=(n // tm,),
        in_specs=[
            pl.BlockSpec((tm, D_MODEL), lambda i: (i, 0)),
            pl.BlockSpec((1, D_MODEL), const),
            pl.BlockSpec((1, D_MODEL), const),
            pl.BlockSpec((D_MODEL, D_FF), const),
            pl.BlockSpec((D_MODEL, D_FF), const),
            pl.BlockSpec((D_FF, D_MODEL), const),
        ],
        out_specs=pl.BlockSpec((tm, D_MODEL), lambda i: (i, 0)),
        out_shape=jax.ShapeDtypeStruct((n, D_MODEL), F32),
        scratch_shapes=[pltpu.VMEM((tm, D_MODEL), F32)],
        compiler_params=_params("parallel"),
        name="ffn",
    )(x, g_in, g_out, w_gate, w_up, w_down)


def _inproj_body(x_ref, g_ref, w_ref, z_ref):
    h = _rms_norm(x_ref[...], g_ref[...]).astype(BF16)
    z_ref[...] = jnp.dot(h, w_ref[...], preferred_element_type=F32)


def _inproj(x, g, w):
    n = x.shape[0]
    tm = min(ROW_TILE, n)
    return pl.pallas_call(
        _inproj_body,
        grid=(n // tm,),
        in_specs=[
            pl.BlockSpec((tm, D_MODEL), lambda i: (i, 0)),
            pl.BlockSpec((1, D_MODEL), lambda i: (0, 0)),
            pl.BlockSpec((D_MODEL, Z_COLS), lambda i: (0, 0)),
        ],
        out_specs=pl.BlockSpec((tm, Z_COLS), lambda i: (i, 0)),
        out_shape=jax.ShapeDtypeStruct((n, Z_COLS), F32),
        compiler_params=_params("parallel"),
        name="inproj",
    )(x, g, w)


def _chunk_index(nc, reverse):
    if reverse:
        return lambda c: nc - 1 - c
    return lambda c: c


def _seq_block(batch, rows, width, nc, reverse, col):
    cidx = _chunk_index(nc, reverse)
    return pl.BlockSpec((batch, rows, width), lambda c: (0, cidx(c), col))


def _const_block(shape):
    return pl.BlockSpec(shape, lambda c: (0,) * len(shape))


def _ret_body(*refs, reverse, nb):
    if reverse:
        (lgs_ref, q_ref, k_ref, v_ref, cos_ref, sin_ref, lgv_ref, bd_ref, bdh_ref, g_ref, fwd_ref,
         o_ref, st_ref) = refs
    else:
        (lgs_ref, q_ref, k_ref, v_ref, cos_ref, sin_ref, lgv_ref, bd_ref, bdh_ref,
         o_ref, st_ref) = refs
    n = RET_CHUNK

    @pl.when(pl.program_id(0) == 0)
    def _():
        st_ref[...] = jnp.zeros_like(st_ref)

    lane = lax.broadcasted_iota(jnp.int32, (1, GROUP), 1)
    first_half = (lane % HEAD_DIM) < (HEAD_DIM // 2)
    cos = cos_ref[...]
    sin = sin_ref[...]

    def rot(t):
        swapped = jnp.where(first_half, pltpu.roll(t, GROUP - HEAD_DIM // 2, 1),
                            pltpu.roll(t, HEAD_DIM // 2, 1))
        return t * cos + swapped * sin

    lg = lgv_ref[...]
    ti = lax.broadcasted_iota(jnp.int32, (n, 1), 0).astype(F32)
    if reverse:
        q_exp, k_exp = n - ti, ti
    else:
        q_exp, k_exp = ti + 1.0, n - 1.0 - ti
    q_dec = jnp.exp(q_exp * lg)
    k_dec = jnp.exp(k_exp * lg)
    c_dec = jnp.exp(float(n) * lg)
    row = lax.broadcasted_iota(jnp.int32, (n, n), 0)
    col = lax.broadcasted_iota(jnp.int32, (n, n), 1)
    if reverse:
        rel, msk = (col - row).astype(F32), col > row
    else:
        rel, msk = (row - col).astype(F32), col <= row
    intra = [jnp.where(msk, jnp.exp(jnp.where(msk, rel, 0.0) * lgs_ref[h]), 0.0) for h in range(N_HEADS)]
    sub_masks = _sub_head_masks(BF16)
    units, chains = _units_and_chains(nb)

    q = [(rot(q_ref[b]) * HEAD_DIM ** -0.5).astype(BF16) for b in range(nb)]
    k_rot = [rot(k_ref[b]) for b in range(nb)]
    k = [t.astype(BF16) for t in k_rot]
    k_scaled = [(t * k_dec).astype(BF16) for t in k_rot]
    v = [v_ref[b].astype(BF16) for b in range(nb)]

    state = {u: st_ref[u[0], u[1]] for u in units}
    inter = {(b, h): _mm(q[b][:, _lanes_of(h)], state[(b, h)]) for (b, h) in units}
    s = {(b, h, j): _mm_nt(q[b][:, _lanes_of(h)] * sub_masks[j], k[b][:, _lanes_of(h)])
         for (b, h, j) in chains}
    p = {(b, h, j): s[(b, h, j)] * intra[h * HEADS_PER_HALF + j] for (b, h, j) in chains}
    o_c = {(b, h, j): _mm(p[(b, h, j)], v[b][:, _lanes_of(h)] * sub_masks[j]) for (b, h, j) in chains}
    for (b, h) in units:
        sl = _lanes_of(h)
        st_ref[b, h] = c_dec[:, sl] * state[(b, h)] + bdh_ref[...] * _mm_tn(k_scaled[b][:, sl], v[b][:, sl])
    for b in range(nb):
        intra_out = {(b, h): o_c[(b, h, 0)] + o_c[(b, h, 1)] for h in range(N_HALVES)}
        out = q_dec * _join_halves(inter, b) + _join_halves(intra_out, b)
        if reverse:
            tot = fwd_ref[b] + out
            ms = _mm_sel(tot * tot, bd_ref[...]) * (1.0 / HEAD_DIM)
            g = g_ref[b]
            o_ref[b] = g * _sigmoid(g) * (tot * lax.rsqrt(ms + RMS_EPS))
        else:
            o_ref[b] = out


def _retention(z3, cos_t, sin_t, log_gamma, bd, bdh):
    batch, seq, _ = z3.shape
    n = RET_CHUNK
    nc = seq // n
    blk = lambda col, rev: _seq_block(batch, n, GROUP, nc, rev, col)

    def tab(rev):
        cidx = _chunk_index(nc, rev)
        return pl.BlockSpec((n, GROUP), lambda c: (cidx(c), 0))

    smem = pl.BlockSpec(memory_space=pltpu.SMEM)
    lgv = jnp.repeat(log_gamma, HEAD_DIM, axis=1)
    common = dict(grid=(nc,), out_shape=jax.ShapeDtypeStruct((batch, seq, GROUP), F32),
                  scratch_shapes=[pltpu.VMEM((batch, N_HALVES, HALF, HALF), F32)],
                  compiler_params=_params("arbitrary"))

    def specs(rev):
        return [smem, blk(ZB_Q, rev), blk(ZB_K, rev), blk(ZB_V, rev), tab(rev), tab(rev),
                _const_block((1, GROUP)), _const_block((GROUP, GROUP)), _const_block((HALF, HALF))]

    fwd = pl.pallas_call(
        functools.partial(_ret_body, reverse=False, nb=batch),
        in_specs=specs(False), out_specs=blk(0, False), name="ret_fwd", **common,
    )(log_gamma[0], z3, z3, z3, cos_t, sin_t, lgv[0:1], bd, bdh)
    return pl.pallas_call(
        functools.partial(_ret_body, reverse=True, nb=batch),
        in_specs=specs(True) + [blk(ZB_G, True), blk(0, True)],
        out_specs=blk(0, True), name="ret_bwd", **common,
    )(log_gamma[1], z3, z3, z3, cos_t, sin_t, lgv[1:2], bd, bdh, z3, fwd)


def _mlstm_body(*refs, reverse, direction, nb):
    if reverse:
        (q_ref, k_ref, v_ref, gi_ref, gf_ref, ib_ref, fb_ref, ed_ref, bd_ref, bdh_ref, o_gate_ref, fwd_ref,
         nw_ref, o_ref, c_ref, n_ref, m_ref) = refs
    else:
        (q_ref, k_ref, v_ref, gi_ref, gf_ref, ib_ref, fb_ref, ed_ref, bd_ref, bdh_ref,
         o_ref, c_ref, n_ref, m_ref) = refs
    n = RET_CHUNK

    @pl.when(pl.program_id(0) == 0)
    def _():
        c_ref[...] = jnp.zeros_like(c_ref)
        n_ref[...] = jnp.zeros_like(n_ref)
        m_ref[...] = jnp.zeros_like(m_ref)

    incl, _ = _order_masks(n, reverse)
    sub_bf = _sub_head_masks(BF16)
    sub_f32 = _sub_head_masks(F32)
    units, chains = _units_and_chains(nb)
    ed = ed_ref[...]
    last = 0 if reverse else n - 1

    qf = [q_ref[b] * HEAD_DIM ** -0.5 for b in range(nb)]
    q = [t.astype(BF16) for t in qf]
    kf = [k_ref[b] for b in range(nb)]
    k = [t.astype(BF16) for t in kf]
    v = [v_ref[b].astype(BF16) for b in range(nb)]
    gi = [gi_ref[b] + ib_ref[...] for b in range(nb)]
    lf = [-_softplus(-(gf_ref[b] + fb_ref[...])) for b in range(nb)]
    bcum = [_sel_mm(incl, lf[b]) for b in range(nb)]
    row_terms = [(gi[b] - bcum[b]).T for b in range(nb)]
    li_x = [_mm_sel(gi[b], ed) for b in range(nb)]
    bcum_x = [_mm_sel(bcum[b], ed) for b in range(nb)]
    m_x = [m_ref[b] for b in range(nb)]
    n_x = [n_ref[b] for b in range(nb)]

    mt, dm = {}, {}
    for (b, h, j) in chains:
        head = h * HEADS_PER_HALF + j
        g_lane = direction * N_HEADS + head
        colv = bcum[b][:, g_lane:g_lane + 1]
        rowv = row_terms[b][g_lane:g_lane + 1, :]
        m_prev = m_x[b][:, head * HEAD_DIM:head * HEAD_DIM + 1]
        d_log = jnp.where(incl, colv + rowv, -jnp.inf)
        mt[(b, h, j)] = jnp.maximum(colv + m_prev, jnp.max(d_log, axis=1, keepdims=True))
        dm[(b, h, j)] = jnp.exp(d_log - mt[(b, h, j)])

    cmat = {u: c_ref[u[0], u[1]] for u in units}
    s = {(b, h, j): _mm_nt(q[b][:, _lanes_of(h)] * sub_bf[j], k[b][:, _lanes_of(h)]) * dm[(b, h, j)]
         for (b, h, j) in chains}
    num_c = {(b, h, j): _mm(s[(b, h, j)], v[b][:, _lanes_of(h)] * sub_bf[j]) for (b, h, j) in chains}
    inter_q = {(b, h): _mm(q[b][:, _lanes_of(h)], cmat[(b, h)]) for (b, h) in units}
    q_dot_n = [_mm_sel(qf[b] * n_x[b], bd_ref[...]) for b in range(nb)]

    num_u, den_u, mt_u = {}, {}, {}
    for (b, h) in units:
        c0, c1 = (b, h, 0), (b, h, 1)
        num_u[(b, h)] = num_c[c0] + num_c[c1]
        den_u[(b, h)] = (jnp.sum(s[c0], axis=1, keepdims=True) * sub_f32[0]
                         + jnp.sum(s[c1], axis=1, keepdims=True) * sub_f32[1])
        mt_u[(b, h)] = mt[c0] * sub_f32[0] + mt[c1] * sub_f32[1]

    for b in range(nb):
        mt_x = _join_halves(mt_u, b)
        inter_w = jnp.exp(bcum_x[b] + m_x[b] - mt_x)
        num = _join_halves(num_u, b) + inter_w * _join_halves(inter_q, b)
        den = _join_halves(den_u, b) + inter_w * q_dot_n[b]
        out = num / jnp.maximum(jnp.abs(den), jnp.exp(-mt_x))
        b_end = bcum_x[b][last:last + 1, :]
        w_log = b_end - bcum_x[b] + li_x[b]
        m_new = jnp.maximum(b_end + m_x[b], jnp.max(w_log, axis=0, keepdims=True))
        old_w = jnp.exp(b_end + m_x[b] - m_new)
        new_w = jnp.exp(w_log - m_new)
        k_w = (kf[b] * new_w).astype(BF16)
        for h in range(N_HALVES):
            sl = _lanes_of(h)
            c_ref[b, h] = old_w[:, sl] * cmat[(b, h)] + bdh_ref[...] * _mm_tn(k_w[:, sl], v[b][:, sl])
        n_ref[b] = old_w * n_x[b] + jnp.sum(new_w * kf[b], axis=0, keepdims=True)
        m_ref[b] = m_new
        if reverse:
            tot = fwd_ref[b] + out
            ms = _mm_sel(tot * tot, bd_ref[...]) * (1.0 / HEAD_DIM)
            o_ref[b] = _sigmoid(o_gate_ref[b]) * (tot * lax.rsqrt(ms + RMS_EPS)) * nw_ref[...]
        else:
            o_ref[b] = out


def _mlstm(z3, i_bias, f_bias, norm_w, expand, bd, bdh):
    batch, seq, _ = z3.shape
    n = RET_CHUNK
    nc = seq // n
    blk = lambda col, rev: _seq_block(batch, n, GROUP, nc, rev, col)
    gblk = lambda col, rev: _seq_block(batch, n, LANES_V7X, nc, rev, col)
    common = dict(grid=(nc,), out_shape=jax.ShapeDtypeStruct((batch, seq, GROUP), F32),
                  scratch_shapes=[pltpu.VMEM((batch, N_HALVES, HALF, HALF), F32),
                                  pltpu.VMEM((batch, 1, GROUP), F32), pltpu.VMEM((batch, 1, GROUP), F32)],
                  compiler_params=_params("arbitrary"))

    def specs(rev):
        return [blk(ZD_Q, rev), blk(ZD_K, rev), blk(ZD_V, rev), gblk(ZD_GI_128, rev), gblk(ZD_GF_128, rev),
                _const_block((1, LANES_V7X)), _const_block((1, LANES_V7X)),
                _const_block((LANES_V7X, GROUP)), _const_block((GROUP, GROUP)), _const_block((HALF, HALF))]

    fwd = pl.pallas_call(
        functools.partial(_mlstm_body, reverse=False, direction=0, nb=batch),
        in_specs=specs(False), out_specs=blk(0, False), name="mlstm_fwd", **common,
    )(z3, z3, z3, z3, z3, i_bias, f_bias, expand[0], bd, bdh)
    return pl.pallas_call(
        functools.partial(_mlstm_body, reverse=True, direction=1, nb=batch),
        in_specs=specs(True) + [blk(ZD_O, True), blk(0, True), _const_block((1, GROUP))],
        out_specs=blk(0, True), name="mlstm_bwd", **common,
    )(z3, z3, z3, z3, z3, i_bias, f_bias, expand[1], bd, bdh, z3, fwd, norm_w)


def _rwkv_body(*refs, reverse, nc, nb):
    if reverse:
        (z_ref, zp_ref, zn_ref, mu_ref, w0_ref, w2_ref, a0_ref, a2_ref, kk_ref, ka_ref, rk_ref, bd_ref,
         bdh_ref, g2_ref, lnw_ref, lnb_ref, wkvf_ref, bonf_ref, o_ref, s_ref) = refs
    else:
        (z_ref, zp_ref, zn_ref, mu_ref, w0_ref, w2_ref, a0_ref, a2_ref, kk_ref, ka_ref, rk_ref, bd_ref,
         bdh_ref, wkv_ref, bon_ref, s_ref) = refs
    step = pl.program_id(0)
    cidx = (nc - 1 - step) if reverse else step

    @pl.when(step == 0)
    def _():
        s_ref[...] = jnp.zeros_like(s_ref)

    n = RWKV_CHUNK
    incl, strict = _order_masks(n, reverse)
    sub_masks = _sub_head_masks(BF16)
    units, chains = _units_and_chains(nb)

    prep = [_rwkv_prepare(z_ref[b], zp_ref[b], zn_ref[b], cidx, nc, reverse, incl, mu_ref, w0_ref, w2_ref,
                          a0_ref, a2_ref, kk_ref, ka_ref, rk_ref, bd_ref) for b in range(nb)]
    operand = lambda name, b, half: prep[b][name][:, _lanes_of(half)]

    state = {u: s_ref[u[0], u[1]] for u in units}
    xo = {u: _mm_nt(operand("ar", *u), state[u]) for u in units}
    ar_m = {c: operand("ar", c[0], c[1]) * sub_masks[c[2]] for c in chains}
    v_m = {c: operand("v", c[0], c[1]) * sub_masks[c[2]] for c in chains}
    sb = {c: _mm_nt(ar_m[c], operand("b_t", c[0], c[1])) for c in chains}
    sk = {c: _mm_nt(ar_m[c], operand("k_t", c[0], c[1])) for c in chains}
    lak_v = {c: _mm(jnp.where(strict, sk[c][0:n], 0.0), v_m[c]) for c in chains}
    x = {u: xo[u][0:n] + lak_v[u + (0,)] + lak_v[u + (1,)] for u in units}

    p = {c: jnp.where(strict, sb[c][0:n], 0.0).astype(BF16) for c in chains}
    u_c = {c: x[c[0:2]] * sub_masks[c[2]].astype(F32) for c in chains}
    u_c = {c: u_c[c] + _mm(p[c], u_c[c]) for c in chains}
    span = 1
    while 2 * span < n:
        p = {c: _mm(p[c], p[c]).astype(BF16) for c in chains}
        u_c = {c: u_c[c] + _mm(p[c], u_c[c]) for c in chains}
        span *= 2

    o_u = {c: _mm(jnp.where(incl, sb[c][n:2 * n], 0.0), u_c[c]) for c in chains}
    o_v = {c: _mm(jnp.where(incl, sk[c][n:2 * n], 0.0), v_m[c]) for c in chains}
    for u in units:
        b, half = u
        u_all = u_c[u + (0,)] + u_c[u + (1,)]
        upd = _mm_tn(u_all, operand("b_t", b, half)) + _mm_tn(operand("v", b, half), operand("k_t", b, half))
        s_ref[b, half] = prep[b]["gam_end"][:, _lanes_of(half)] * (state[u] + bdh_ref[...] * upd)

    wkv_u = {u: xo[u][n:2 * n] + o_u[u + (0,)] + o_u[u + (1,)] + o_v[u + (0,)] + o_v[u + (1,)] for u in units}
    for b in range(nb):
        wkv = _join_halves(wkv_u, b)
        bonus = prep[b]["bonus"]
        if reverse:
            bd = bd_ref[...]
            wkv = wkvf_ref[b] + wkv
            bon = bonf_ref[b] + bonus
            mean = _mm_sel(wkv, bd) * (1.0 / HEAD_DIM)
            cen = wkv - mean
            var = _mm_sel(cen * cen, bd) * (1.0 / HEAD_DIM)
            out = cen * lax.rsqrt(var + RWKV_GN_EPS) * lnw_ref[...] + lnb_ref[...] + bon
            o_ref[b] = out * _mm(_sigmoid(prep[b]["lat2"]), g2_ref[...])
        else:
            wkv_ref[b] = wkv
            bon_ref[b] = bonus


def _rwkv_prepare(z, zp, zn, cidx, nc, reverse, incl, mu_ref, w0_ref, w2_ref, a0_ref, a2_ref, kk_ref,
                  ka_ref, rk_ref, bd_ref):
    n = RWKV_CHUNK
    z_prev = jnp.where(cidx > 0, zp[SUBLANES_V7X - 1:SUBLANES_V7X, :], 0.0)
    z_next = jnp.where(cidx < nc - 1, zn[0:1, :], 0.0)
    ridx = lax.broadcasted_iota(jnp.int32, (n, 1), 0)
    z_m1 = jnp.where(ridx == 0, z_prev, pltpu.roll(z, 1, 0))
    z_p1 = jnp.where(ridx == n - 1, z_next, pltpu.roll(z, n - 1, 0))
    zs = z + mu_ref[...] * (0.5 * (z_m1 + z_p1) - z)
    r = zs[:, 0:GROUP]
    k = zs[:, GROUP:2 * GROUP]
    v = zs[:, 2 * GROUP:3 * GROUP]
    lat1 = zs[:, 3 * GROUP:3 * GROUP + LANES_V7X]
    lat2 = zs[:, 3 * GROUP + LANES_V7X:4 * GROUP]

    w_pre = w0_ref[...] + _mm(jnp.tanh(lat1), w2_ref[...])
    log_decay = -jnp.exp(-_softplus(-w_pre) - 0.5)
    a = _sigmoid(a0_ref[...] + _mm(lat1, a2_ref[...]))
    kk_raw = k * kk_ref[...]
    kk = kk_raw / jnp.maximum(jnp.sqrt(_mm_sel(kk_raw * kk_raw, bd_ref[...])), 1e-12)
    kd = k * (1.0 + (a - 1.0) * ka_ref[...])
    bonus = _mm_sel(r * kd * rk_ref[...], bd_ref[...]) * v

    cum_incl = _sel_mm(incl, log_decay)
    cum_excl = cum_incl - log_decay
    last = 0 if reverse else n - 1
    a_t = -kk * jnp.exp(cum_excl)
    r_t = r * jnp.exp(cum_incl)
    inv_g = jnp.exp(-cum_incl)
    return dict(ar=jnp.concatenate([a_t, r_t], axis=0).astype(BF16),
                b_t=(kk * a * inv_g).astype(BF16), k_t=(kd * inv_g).astype(BF16), v=v.astype(BF16),
                gam_end=jnp.exp(cum_incl[last:last + 1, :]), bonus=bonus, lat2=lat2)


def _rwkv(z3, p, bd, bdh):
    batch, seq, _ = z3.shape
    n = RWKV_CHUNK
    nc = seq // n
    rows8 = n // SUBLANES_V7X
    last8 = seq // SUBLANES_V7X - 1
    wide = 4 * GROUP

    def halo(rev, offset):
        cidx = _chunk_index(nc, rev)
        return pl.BlockSpec((batch, SUBLANES_V7X, wide), lambda c: (
            0, jnp.clip((cidx(c) + offset) * rows8 - (1 - offset), 0, last8), ZC_BLOCK_1024))

    vec = _const_block((1, GROUP))
    lora = _const_block((LANES_V7X, GROUP))
    out_blk = lambda rev: _seq_block(batch, n, GROUP, nc, rev, 0)
    out_shape = jax.ShapeDtypeStruct((batch, seq, GROUP), F32)

    def specs(rev):
        return [_seq_block(batch, n, wide, nc, rev, ZC_BLOCK_1024), halo(rev, 0), halo(rev, 1),
                _const_block((1, wide)), vec, lora, vec, lora, vec, vec, vec,
                _const_block((GROUP, GROUP)), _const_block((HALF, HALF))]

    common = dict(grid=(nc,), scratch_shapes=[pltpu.VMEM((batch, N_HALVES, HALF, HALF), F32)],
                  compiler_params=_params("arbitrary"))
    wkv_f, bon_f = pl.pallas_call(
        functools.partial(_rwkv_body, reverse=False, nc=nc, nb=batch),
        in_specs=specs(False), out_specs=[out_blk(False), out_blk(False)],
        out_shape=[out_shape, out_shape], name="rwkv_fwd", **common,
    )(z3, z3, z3, p["mu"], p["w0"][0], p["w2"][0], p["a0"][0], p["a2"][0], p["k_k"], p["k_a"], p["r_k"],
      bd, bdh)
    return pl.pallas_call(
        functools.partial(_rwkv_body, reverse=True, nc=nc, nb=batch),
        in_specs=specs(True) + [lora, vec, vec, out_blk(True), out_blk(True)],
        out_specs=out_blk(True), out_shape=out_shape, name="rwkv_bwd", **common,
    )(z3, z3, z3, p["mu"], p["w0"][1], p["w2"][1], p["a0"][1], p["a2"][1], p["k_k"], p["k_a"], p["r_k"],
      bd, bdh, p["g2"], p["ln_w"], p["ln_b"], wkv_f, bon_f)


def _outproj_body(x_ref, cb_ref, cc_ref, ch_ref, ccp_ref, chp_ref, ccn_ref, chn_ref, yb_ref, yc_ref, yd_ref,
                  cw_ref, cbias_ref, w_ref, g_ref, o_ref, *, tiles_per_seq):
    tm = x_ref.shape[0]
    t = pl.program_id(0) % tiles_per_seq
    u = cc_ref[...] * ch_ref[...]
    last8 = SUBLANES_V7X - 1
    u_prev = jnp.where(t > 0, ccp_ref[last8:last8 + 1, :] * chp_ref[last8:last8 + 1, :], 0.0)
    u_next = jnp.where(t < tiles_per_seq - 1, ccn_ref[0:1, :] * chn_ref[0:1, :], 0.0)
    ridx = lax.broadcasted_iota(jnp.int32, (tm, 1), 0)
    u_m1 = jnp.where(ridx == 0, u_prev, pltpu.roll(u, 1, 0))
    u_p1 = jnp.where(ridx == tm - 1, u_next, pltpu.roll(u, tm - 1, 0))
    cw = cw_ref[...]
    conv = cw[0:1, :] * u_m1 + cw[1:2, :] * u + cw[2:3, :] * u_p1 + cbias_ref[...]
    y_conv = cb_ref[...] * conv
    y = (_mm(y_conv, w_ref[0:GROUP, :]) + _mm(yb_ref[...], w_ref[GROUP:2 * GROUP, :])
         + _mm(yc_ref[...], w_ref[2 * GROUP:3 * GROUP, :]) + _mm(yd_ref[...], w_ref[3 * GROUP:4 * GROUP, :]))
    o_ref[...] = x_ref[...] + _rms_norm(y, g_ref[...])


def _outproj(x, z, y_ret, y_rwkv, y_mlstm, conv_w8, conv_b, w_out, g, seq):
    n = x.shape[0]
    tm = min(ROW_TILE, seq)
    tiles_per_seq = seq // tm
    rows8 = tm // SUBLANES_V7X
    last8 = n // SUBLANES_V7X - 1
    zblk = lambda col: pl.BlockSpec((tm, GROUP), lambda i: (i, col))
    prev = lambda col: pl.BlockSpec((SUBLANES_V7X, GROUP), lambda i: (jnp.maximum(i * rows8 - 1, 0), col))
    nxt = lambda col: pl.BlockSpec((SUBLANES_V7X, GROUP), lambda i: (jnp.minimum((i + 1) * rows8, last8), col))
    yblk = pl.BlockSpec((tm, GROUP), lambda i: (i, 0))
    const = lambda i: (0, 0)
    return pl.pallas_call(
        functools.partial(_outproj_body, tiles_per_seq=tiles_per_seq),
        grid=(n // tm,),
        in_specs=[pl.BlockSpec((tm, D_MODEL), lambda i: (i, 0)),
                  zblk(ZA_B), zblk(ZA_C), zblk(ZA_H), prev(ZA_C), prev(ZA_H), nxt(ZA_C), nxt(ZA_H),
                  yblk, yblk, yblk,
                  pl.BlockSpec((SUBLANES_V7X, GROUP), const), pl.BlockSpec((1, GROUP), const),
                  pl.BlockSpec((D_MODEL, D_MODEL), const), pl.BlockSpec((1, D_MODEL), const)],
        out_specs=pl.BlockSpec((tm, D_MODEL), lambda i: (i, 0)),
        out_shape=jax.ShapeDtypeStruct((n, D_MODEL), F32),
        compiler_params=_params("parallel"),
        name="outproj",
    )(x, z, z, z, z, z, z, z, y_ret, y_rwkv, y_mlstm, conv_w8, conv_b, w_out, g)


def _relayout_w_in(w):
    pad = lambda c: jnp.zeros((D_MODEL, c), w.dtype)
    a, b = w[:, 0:768], w[:, 768:1792]
    c = w[:, 1792:1792 + RWKV_COLS]
    d0 = 1792 + RWKV_COLS
    d = w[:, d0:d0 + 4 * GROUP]
    gi = w[:, d0 + 4 * GROUP:d0 + 4 * GROUP + 2 * N_HEADS]
    gf = w[:, d0 + 4 * GROUP + 2 * N_HEADS:d0 + 4 * GROUP + 4 * N_HEADS]
    gpad = LANES_V7X - 2 * N_HEADS
    return jnp.concatenate([b, c, pad(4 * GROUP - RWKV_COLS), a, d, gi, pad(gpad), gf, pad(gpad)],
                           axis=1).astype(BF16)


def _block_diag_ones():
    idx = np.arange(GROUP) // HEAD_DIM
    return jnp.asarray((idx[:, None] == idx[None, :]).astype(np.float32))


def _gate_expand():
    j = np.arange(LANES_V7X)[:, None]
    head = np.arange(GROUP)[None, :] // HEAD_DIM
    return jnp.asarray(np.stack([(j == d * N_HEADS + head) for d in range(2)]).astype(np.float32))


def _rope_tables(seq):
    inv = 10000.0 ** (-jnp.arange(0, HEAD_DIM, 2, dtype=F32) / HEAD_DIM)
    ang = jnp.arange(seq, dtype=F32)[:, None] * inv[None, :]
    cos, sin = jnp.cos(ang), jnp.sin(ang)
    cos_t = jnp.tile(jnp.concatenate([cos, cos], axis=1), (1, N_HEADS))
    sin_t = jnp.tile(jnp.concatenate([-sin, sin], axis=1), (1, N_HEADS))
    return cos_t, sin_t


def _lane_pad(v, width):
    v = v.reshape(1, -1)
    return jnp.pad(v, ((0, 0), (0, width - v.shape[1])))


def _rwkv_params(mu, w0, w2, a0, a2, g2, k_k, k_a, r_k, ln_w, ln_b):
    rank_w, rank_a, rank_g = w2.shape[1], a2.shape[1], g2.shape[0]
    w2p = jnp.zeros((2, LANES_V7X, GROUP), F32)
    a2p = jnp.zeros((2, LANES_V7X, GROUP), F32)
    for d in range(2):
        w2p = w2p.at[d, d * rank_w:(d + 1) * rank_w].set(w2[d])
        a2p = a2p.at[d, 2 * rank_w + d * rank_a:2 * rank_w + (d + 1) * rank_a].set(a2[d])
    g2p = jnp.zeros((LANES_V7X, GROUP), F32).at[0:rank_g].set(g2)
    return dict(mu=_lane_pad(mu, 4 * GROUP), w0=w0.reshape(2, 1, GROUP), w2=w2p.astype(BF16),
                a0=a0.reshape(2, 1, GROUP), a2=a2p.astype(BF16), g2=g2p.astype(BF16),
                k_k=k_k.reshape(1, GROUP), k_a=k_a.reshape(1, GROUP), r_k=r_k.reshape(1, GROUP),
                ln_w=ln_w.reshape(1, GROUP), ln_b=ln_b.reshape(1, GROUP))


def kernel(x, norm_g, ffn_w_gate, ffn_w_up, ffn_w_down, w_in, w_out, conv_w, conv_b, ret_decay_logit, rwkv_mu, rwkv_w0, rwkv_w2, rwkv_a0, rwkv_a2, rwkv_g2, rwkv_k_k, rwkv_k_a, rwkv_r_k, rwkv_ln_w, rwkv_ln_b, mlstm_i_bias, mlstm_f_bias, mlstm_norm_w):
    batch, seq, _ = x.shape
    depth = norm_g.shape[0]
    n_tok = batch * seq
    assert seq % ROW_TILE == 0 and seq % RET_CHUNK == 0 and seq % RWKV_CHUNK == 0
    cos_t, sin_t = _rope_tables(seq)
    bd = _block_diag_ones()
    bdh = bd[0:HALF, 0:HALF]
    expand = _gate_expand()
    xf = x.reshape(n_tok, D_MODEL)
    for l in range(depth):
        g = norm_g[l].reshape(6, 1, D_MODEL)
        xf = _ffn(xf, g[0], g[1], ffn_w_gate[l, 0].astype(BF16), ffn_w_up[l, 0].astype(BF16),
                  ffn_w_down[l, 0].astype(BF16))
        z = _inproj(xf, g[2], _relayout_w_in(w_in[l]))
        z3 = z.reshape(batch, seq, Z_COLS)
        log_gamma = jax.nn.log_sigmoid(ret_decay_logit[l].astype(F32))
        y_ret = _retention(z3, cos_t, sin_t, log_gamma, bd, bdh)
        y_rwkv = _rwkv(z3, _rwkv_params(
            rwkv_mu[l], rwkv_w0[l], rwkv_w2[l], rwkv_a0[l], rwkv_a2[l], rwkv_g2[l], rwkv_k_k[l],
            rwkv_k_a[l], rwkv_r_k[l], rwkv_ln_w[l], rwkv_ln_b[l]), bd, bdh)
        y_mlstm = _mlstm(z3, _lane_pad(mlstm_i_bias[l], LANES_V7X), _lane_pad(mlstm_f_bias[l], LANES_V7X),
                         mlstm_norm_w[l].reshape(1, GROUP), expand, bd, bdh)
        conv_w8 = jnp.pad(conv_w[l], ((0, SUBLANES_V7X - conv_w.shape[1]), (0, 0)))
        xf = _outproj(xf, z, y_ret.reshape(n_tok, GROUP), y_rwkv.reshape(n_tok, GROUP),
                      y_mlstm.reshape(n_tok, GROUP), conv_w8, conv_b[l].reshape(1, GROUP),
                      w_out[l].astype(BF16), g[3], seq)
        xf = _ffn(xf, g[4], g[5], ffn_w_gate[l, 1].astype(BF16), ffn_w_up[l, 1].astype(BF16),
                  ffn_w_down[l, 1].astype(BF16))
    return xf.reshape(batch, seq, D_MODEL)
```

```python
import functools

import numpy as np
import jax
import jax.numpy as jnp
from jax import lax
from jax.experimental import pallas as pl
from jax.experimental.pallas import tpu as pltpu

F32 = jnp.float32
BF16 = jnp.bfloat16

D_MODEL = 1024
GROUP = 256
N_HEADS = 4
HEAD_DIM = 64
D_FF = 2816
RWKV_COLS = 960
RMS_EPS = 1e-6
RWKV_GN_EPS = 64e-5

LANES_V7X = 128
SUBLANES_V7X = 8
VMEM_LIMIT_BYTES = 56 * 1024 * 1024

ROW_TILE = 512
FFN_CHUNK = 256
RET_CHUNK = 128
RWKV_CHUNK = 64
RWKV_SUBCHUNKS = 4

HALF = LANES_V7X
N_HALVES = GROUP // HALF
HEADS_PER_HALF = HALF // HEAD_DIM

Z_COLS = 4096
ZB_Q, ZB_K, ZB_V, ZB_G = 0, 1, 2, 3
ZC_BLOCK_1024 = 1
ZA_B, ZA_C, ZA_H = 8, 9, 10
ZD_Q, ZD_K, ZD_V, ZD_O = 11, 12, 13, 14
ZD_GI_128, ZD_GF_128 = 30, 31


def _mm(a, b):
    return jnp.dot(a.astype(BF16), b.astype(BF16), preferred_element_type=F32)


def _mm_nt(a, b):
    return lax.dot_general(a.astype(BF16), b.astype(BF16), (((1,), (1,)), ((), ())),
                           preferred_element_type=F32)


def _mm_tn(a, b):
    return lax.dot_general(a.astype(BF16), b.astype(BF16), (((0,), (0,)), ((), ())),
                           preferred_element_type=F32)


def _bf16_terms(x, terms):
    out = []
    for _ in range(terms - 1):
        part = x.astype(BF16)
        out.append(part)
        x = x - part.astype(F32)
    out.append(x.astype(BF16))
    return out


def _sel_mm(sel, x, terms=3):
    s = sel.astype(BF16)
    return sum(jnp.dot(s, part, preferred_element_type=F32) for part in _bf16_terms(x, terms))


def _mm_sel(x, sel, terms=3):
    s = sel.astype(BF16)
    return sum(jnp.dot(part, s, preferred_element_type=F32) for part in _bf16_terms(x, terms))


def _head_sum(x, bdh):
    return jnp.concatenate([_mm(x[:, _lanes_of(h)], bdh) for h in range(N_HALVES)], axis=1)


def _rows_of(stacked, b, n):
    return stacked[b * n:(b + 1) * n]


def _div_pow2(x, d):
    shift = d.bit_length() - 1
    assert 1 << shift == d
    return lax.shift_right_logical(x, shift)


def _rms_norm(x, g):
    return x * lax.rsqrt(jnp.mean(x * x, axis=-1, keepdims=True) + RMS_EPS) * g


def _softplus(x):
    return jnp.maximum(x, 0.0) + jnp.log(1.0 + jnp.exp(-jnp.abs(x)))


def _sigmoid(x):
    return 1.0 / (1.0 + jnp.exp(-x))


def _order_masks(n, reverse):
    row = lax.broadcasted_iota(jnp.int32, (n, n), 0)
    col = lax.broadcasted_iota(jnp.int32, (n, n), 1)
    if reverse:
        return col >= row, col > row
    return col <= row, col < row


def _sub_head_masks(dtype):
    lane = lax.broadcasted_iota(jnp.int32, (1, HALF), 1)
    return [((lane >= j * HEAD_DIM) & (lane < (j + 1) * HEAD_DIM)).astype(dtype)
            for j in range(HEADS_PER_HALF)]


def _lanes_of(half):
    return slice(half * HALF, (half + 1) * HALF)


def _units_and_chains(nb):
    units = [(b, half) for b in range(nb) for half in range(N_HALVES)]
    chains = [(b, half, j) for (b, half) in units for j in range(HEADS_PER_HALF)]
    return units, chains


def _join_halves(per_unit, b):
    return jnp.concatenate([per_unit[(b, half)] for half in range(N_HALVES)], axis=1)


def _params(*sem):
    return pltpu.CompilerParams(dimension_semantics=sem, vmem_limit_bytes=VMEM_LIMIT_BYTES)


def _ffn_body(x_ref, gin_ref, gout_ref, wg_ref, wu_ref, wd_ref, o_ref, acc_ref):
    x = x_ref[...]
    h = _rms_norm(x, gin_ref[...]).astype(BF16)
    for c in range(D_FF // FFN_CHUNK):
        lo = c * FFN_CHUNK
        gate = jnp.dot(h, wg_ref[:, lo:lo + FFN_CHUNK], preferred_element_type=F32)
        up = jnp.dot(h, wu_ref[:, lo:lo + FFN_CHUNK], preferred_element_type=F32)
        act = (gate * _sigmoid(gate) * up).astype(BF16)
        part = jnp.dot(act, wd_ref[lo:lo + FFN_CHUNK, :], preferred_element_type=F32)
        if c == 0:
            acc_ref[...] = part
        else:
            acc_ref[...] += part
    o_ref[...] = x + 0.5 * _rms_norm(acc_ref[...], gout_ref[...])


def _ffn(x, g_in, g_out, w_gate, w_up, w_down):
    n = x.shape[0]
    tm = min(ROW_TILE, n)
    const = lambda i: (0, 0)
    return pl.pallas_call(
        _ffn_body,
        grid=(n // tm,),
        in_specs=[
            pl.BlockSpec((tm, D_MODEL), lambda i: (i, 0)),
            pl.BlockSpec((1, D_MODEL), const),
            pl.BlockSpec((1, D_MODEL), const),
            pl.BlockSpec((D_MODEL, D_FF), const),
            pl.BlockSpec((D_MODEL, D_FF), const),
            pl.BlockSpec((D_FF, D_MODEL), const),
        ],
        out_specs=pl.BlockSpec((tm, D_MODEL), lambda i: (i, 0)),
        out_shape=jax.ShapeDtypeStruct((n, D_MODEL), F32),
        scratch_shapes=[pltpu.VMEM((tm, D_MODEL), F32)],
        compiler_params=_params("parallel"),
        name="ffn",
    )(x, g_in, g_out, w_gate, w_up, w_down)


def _inproj_body(x_ref, g_ref, w_ref, z_ref):
    h = _rms_norm(x_ref[...], g_ref[...]).astype(BF16)
    z_ref[...] = jnp.dot(h, w_ref[...], preferred_element_type=F32)


def _inproj(x, g, w):
    n = x.shape[0]
    tm = min(ROW_TILE, n)
    return pl.pallas_call(
        _inproj_body,
        grid=(n // tm,),
        in_specs=[
            pl.BlockSpec((tm, D_MODEL), lambda i: (i, 0)),
            pl.BlockSpec((1, D_MODEL), lambda i: (0, 0)),
            pl.BlockSpec((D_MODEL, Z_COLS), lambda i: (0, 0)),
        ],
        out_specs=pl.BlockSpec((tm, Z_COLS), lambda i: (i, 0)),
        out_shape=jax.ShapeDtypeStruct((n, Z_COLS), F32),
        compiler_params=_params("parallel"),
        name="inproj",
    )(x, g, w)


def _chunk_index(nc, reverse):
    if reverse:
        return lambda c: nc - 1 - c
    return lambda c: c


def _seq_block(batch, rows, width, nc, reverse, col):
    cidx = _chunk_index(nc, reverse)
    return pl.BlockSpec((batch, rows, width), lambda c: (0, cidx(c), col))


def _const_block(shape):
    return pl.BlockSpec(shape, lambda c: (0,) * len(shape))


def _ret_body(*refs, reverse, nb):
    if reverse:
        (lgs_ref, q_ref, k_ref, v_ref, cos_ref, sin_ref, lgv_ref, bdh_ref, g_ref, fwd_ref,
         o_ref, st_ref) = refs
    else:
        lgs_ref, q_ref, k_ref, v_ref, cos_ref, sin_ref, lgv_ref, bdh_ref, o_ref, st_ref = refs
    n = RET_CHUNK

    @pl.when(pl.program_id(0) == 0)
    def _():
        st_ref[...] = jnp.zeros_like(st_ref)

    lane = lax.broadcasted_iota(jnp.int32, (1, GROUP), 1)
    first_half = (lane % HEAD_DIM) < (HEAD_DIM // 2)
    cos = cos_ref[...]
    sin = sin_ref[...]

    def rot(t):
        swapped = jnp.where(first_half, pltpu.roll(t, GROUP - HEAD_DIM // 2, 1),
                            pltpu.roll(t, HEAD_DIM // 2, 1))
        return t * cos + swapped * sin

    lg = lgv_ref[...]
    ti = lax.broadcasted_iota(jnp.int32, (n, 1), 0).astype(F32)
    if reverse:
        q_exp, k_exp = n - ti, ti
    else:
        q_exp, k_exp = ti + 1.0, n - 1.0 - ti
    q_dec = jnp.exp(q_exp * lg)
    k_dec = jnp.exp(k_exp * lg)
    c_dec = jnp.exp(float(n) * lg)
    row = lax.broadcasted_iota(jnp.int32, (n, n), 0)
    col = lax.broadcasted_iota(jnp.int32, (n, n), 1)
    if reverse:
        rel, msk = (col - row).astype(F32), col > row
    else:
        rel, msk = (row - col).astype(F32), col <= row
    intra = [jnp.where(msk, jnp.exp(jnp.where(msk, rel, 0.0) * lgs_ref[h]), 0.0) for h in range(N_HEADS)]
    sub_masks = _sub_head_masks(BF16)
    units, chains = _units_and_chains(nb)

    q = [(rot(q_ref[b]) * HEAD_DIM ** -0.5).astype(BF16) for b in range(nb)]
    k_rot = [rot(k_ref[b]) for b in range(nb)]
    k = [t.astype(BF16) for t in k_rot]
    k_scaled = [(t * k_dec).astype(BF16) for t in k_rot]
    v = [v_ref[b].astype(BF16) for b in range(nb)]

    state = {u: st_ref[u[0], u[1]] for u in units}
    inter = {(b, h): _mm(q[b][:, _lanes_of(h)], state[(b, h)]) for (b, h) in units}
    s = {(b, h, j): _mm_nt(q[b][:, _lanes_of(h)] * sub_masks[j], k[b][:, _lanes_of(h)])
         for (b, h, j) in chains}
    p = {(b, h, j): (s[(b, h, j)] * intra[h * HEADS_PER_HALF + j]).astype(BF16) for (b, h, j) in chains}
    intra_out = {
        (b, h): _mm(jnp.concatenate([p[(b, h, j)] for j in range(HEADS_PER_HALF)], axis=1),
                    jnp.concatenate([v[b][:, _lanes_of(h)] * sub_masks[j] for j in range(HEADS_PER_HALF)],
                                    axis=0))
        for (b, h) in units}
    for (b, h) in units:
        sl = _lanes_of(h)
        st_ref[b, h] = c_dec[:, sl] * state[(b, h)] + bdh_ref[...] * _mm_tn(k_scaled[b][:, sl], v[b][:, sl])
    outs = [q_dec * _join_halves(inter, b) + _join_halves(intra_out, b) for b in range(nb)]
    if reverse:
        tot = jnp.concatenate([fwd_ref[b] + outs[b] for b in range(nb)], axis=0)
        normed = tot * lax.rsqrt(_head_sum(tot * tot, bdh_ref[...]) * (1.0 / HEAD_DIM) + RMS_EPS)
        for b in range(nb):
            g = g_ref[b]
            o_ref[b] = (g * _sigmoid(g) * _rows_of(normed, b, n)).astype(o_ref.dtype)
    else:
        for b in range(nb):
            o_ref[b] = outs[b]


def _retention(z3, cos_t, sin_t, log_gamma, bdh):
    batch, seq, _ = z3.shape
    n = RET_CHUNK
    nc = seq // n
    blk = lambda col, rev: _seq_block(batch, n, GROUP, nc, rev, col)

    def tab(rev):
        cidx = _chunk_index(nc, rev)
        return pl.BlockSpec((n, GROUP), lambda c: (cidx(c), 0))

    smem = pl.BlockSpec(memory_space=pltpu.SMEM)
    lgv = jnp.repeat(log_gamma, HEAD_DIM, axis=1)
    common = dict(grid=(nc,), scratch_shapes=[pltpu.VMEM((batch, N_HALVES, HALF, HALF), F32)],
                  compiler_params=_params("arbitrary"))

    def specs(rev):
        return [smem, blk(ZB_Q, rev), blk(ZB_K, rev), blk(ZB_V, rev), tab(rev), tab(rev),
                _const_block((1, GROUP)), _const_block((HALF, HALF))]

    fwd = pl.pallas_call(
        functools.partial(_ret_body, reverse=False, nb=batch),
        in_specs=specs(False), out_specs=blk(0, False),
        out_shape=jax.ShapeDtypeStruct((batch, seq, GROUP), F32), name="ret_fwd", **common,
    )(log_gamma[0], z3, z3, z3, cos_t, sin_t, lgv[0:1], bdh)
    return pl.pallas_call(
        functools.partial(_ret_body, reverse=True, nb=batch),
        in_specs=specs(True) + [blk(ZB_G, True), blk(0, True)], out_specs=blk(0, True),
        out_shape=jax.ShapeDtypeStruct((batch, seq, GROUP), BF16), name="ret_bwd", **common,
    )(log_gamma[1], z3, z3, z3, cos_t, sin_t, lgv[1:2], bdh, z3, fwd)


def _mlstm_body(*refs, reverse, direction, nb):
    if reverse:
        (q_ref, k_ref, v_ref, gi_ref, gf_ref, ib_ref, fb_ref, ed_ref, bdh_ref, o_gate_ref, fwd_ref,
         nw_ref, o_ref, c_ref, n_ref, m_ref) = refs
    else:
        (q_ref, k_ref, v_ref, gi_ref, gf_ref, ib_ref, fb_ref, ed_ref, bdh_ref,
         o_ref, c_ref, n_ref, m_ref) = refs
    n = RET_CHUNK

    @pl.when(pl.program_id(0) == 0)
    def _():
        c_ref[...] = jnp.zeros_like(c_ref)
        n_ref[...] = jnp.zeros_like(n_ref)
        m_ref[...] = jnp.zeros_like(m_ref)

    incl, _ = _order_masks(n, reverse)
    sub_bf = _sub_head_masks(BF16)
    sub_f32 = _sub_head_masks(F32)
    units, chains = _units_and_chains(nb)
    ed = ed_ref[...]
    last = 0 if reverse else n - 1

    qf = [q_ref[b] * HEAD_DIM ** -0.5 for b in range(nb)]
    q = [t.astype(BF16) for t in qf]
    kf = [k_ref[b] for b in range(nb)]
    k = [t.astype(BF16) for t in kf]
    v = [v_ref[b].astype(BF16) for b in range(nb)]
    gi = [gi_ref[b] + ib_ref[...] for b in range(nb)]
    lf = [-_softplus(-(gf_ref[b] + fb_ref[...])) for b in range(nb)]
    bcum_lanes = _sel_mm(incl, jnp.concatenate(lf, axis=1))
    bcum = [bcum_lanes[:, b * LANES_V7X:(b + 1) * LANES_V7X] for b in range(nb)]
    row_terms = [(gi[b] - bcum[b]).T for b in range(nb)]
    li_all = _mm_sel(jnp.concatenate(gi, axis=0), ed)
    bcum_all = _mm_sel(jnp.concatenate(bcum, axis=0), ed)
    li_x = [_rows_of(li_all, b, n) for b in range(nb)]
    bcum_x = [_rows_of(bcum_all, b, n) for b in range(nb)]
    m_x = [m_ref[b] for b in range(nb)]
    n_x = [n_ref[b] for b in range(nb)]

    head_of = lambda c: c[1] * HEADS_PER_HALF + c[2]
    gate_lane = lambda c: direction * N_HEADS + head_of(c)
    colv = {c: bcum[c[0]][:, gate_lane(c):gate_lane(c) + 1] for c in chains}
    col_b = {c: jnp.broadcast_to(colv[c], (n, n)) for c in chains}
    d_log = {c: jnp.where(incl, col_b[c] + row_terms[c[0]][gate_lane(c):gate_lane(c) + 1, :], -jnp.inf)
             for c in chains}
    row_max = {c: jnp.max(d_log[c], axis=1, keepdims=True) for c in chains}
    mt = {c: jnp.maximum(colv[c] + m_x[c[0]][:, head_of(c) * HEAD_DIM:head_of(c) * HEAD_DIM + 1], row_max[c])
          for c in chains}
    mt_b = {c: jnp.broadcast_to(mt[c], (n, n)) for c in chains}
    dm = {c: jnp.exp(d_log[c] - mt_b[c]) for c in chains}

    cmat = {u: c_ref[u[0], u[1]] for u in units}
    s = {(b, h, j): (_mm_nt(q[b][:, _lanes_of(h)] * sub_bf[j], k[b][:, _lanes_of(h)]) * dm[(b, h, j)]
                     ).astype(BF16) for (b, h, j) in chains}
    ones_rows = _div_pow2(lax.broadcasted_iota(jnp.int32, (HEADS_PER_HALF * n, HALF), 0), n)
    ones_lanes = _div_pow2(lax.broadcasted_iota(jnp.int32, (HEADS_PER_HALF * n, HALF), 1), HEAD_DIM)
    head_ones = (ones_rows == ones_lanes).astype(BF16)
    num_u, den_u = {}, {}
    for (b, h) in units:
        v_h = v[b][:, _lanes_of(h)]
        rhs = jnp.concatenate([jnp.concatenate([v_h * sub_bf[j] for j in range(HEADS_PER_HALF)], axis=0),
                               head_ones], axis=1)
        nd = _mm(jnp.concatenate([s[(b, h, j)] for j in range(HEADS_PER_HALF)], axis=1), rhs)
        num_u[(b, h)] = nd[:, 0:HALF]
        den_u[(b, h)] = nd[:, HALF:2 * HALF]
    inter_q = {(b, h): _mm(q[b][:, _lanes_of(h)], cmat[(b, h)]) for (b, h) in units}
    q_dot_n = _head_sum(jnp.concatenate([qf[b] * n_x[b] for b in range(nb)], axis=0), bdh_ref[...])
    mt_u = {(b, h): mt[(b, h, 0)] * sub_f32[0] + mt[(b, h, 1)] * sub_f32[1] for (b, h) in units}

    outs = []
    for b in range(nb):
        mt_x = _join_halves(mt_u, b)
        inter_w = jnp.exp(bcum_x[b] + m_x[b] - mt_x)
        num = _join_halves(num_u, b) + inter_w * _join_halves(inter_q, b)
        den = _join_halves(den_u, b) + inter_w * _rows_of(q_dot_n, b, n)
        outs.append(num / jnp.maximum(jnp.abs(den), jnp.exp(-mt_x)))
        b_end = bcum_x[b][last:last + 1, :]
        w_log = b_end - bcum_x[b] + li_x[b]
        m_new = jnp.maximum(b_end + m_x[b], jnp.max(w_log, axis=0, keepdims=True))
        old_w = jnp.exp(b_end + m_x[b] - m_new)
        new_w = jnp.exp(w_log - m_new)
        k_w = (kf[b] * new_w).astype(BF16)
        for h in range(N_HALVES):
            sl = _lanes_of(h)
            c_ref[b, h] = old_w[:, sl] * cmat[(b, h)] + bdh_ref[...] * _mm_tn(k_w[:, sl], v[b][:, sl])
        n_ref[b] = old_w * n_x[b] + jnp.sum(new_w * kf[b], axis=0, keepdims=True)
        m_ref[b] = m_new
    if reverse:
        tot = jnp.concatenate([fwd_ref[b] + outs[b] for b in range(nb)], axis=0)
        normed = tot * lax.rsqrt(_head_sum(tot * tot, bdh_ref[...]) * (1.0 / HEAD_DIM) + RMS_EPS)
        for b in range(nb):
            o_ref[b] = (_sigmoid(o_gate_ref[b]) * _rows_of(normed, b, n) * nw_ref[...]).astype(o_ref.dtype)
    else:
        for b in range(nb):
            o_ref[b] = outs[b]


def _mlstm(z3, i_bias, f_bias, norm_w, expand, bdh):
    batch, seq, _ = z3.shape
    n = RET_CHUNK
    nc = seq // n
    blk = lambda col, rev: _seq_block(batch, n, GROUP, nc, rev, col)
    gblk = lambda col, rev: _seq_block(batch, n, LANES_V7X, nc, rev, col)
    common = dict(grid=(nc,),
                  scratch_shapes=[pltpu.VMEM((batch, N_HALVES, HALF, HALF), F32),
                                  pltpu.VMEM((batch, 1, GROUP), F32), pltpu.VMEM((batch, 1, GROUP), F32)],
                  compiler_params=_params("arbitrary"))

    def specs(rev):
        return [blk(ZD_Q, rev), blk(ZD_K, rev), blk(ZD_V, rev), gblk(ZD_GI_128, rev), gblk(ZD_GF_128, rev),
                _const_block((1, LANES_V7X)), _const_block((1, LANES_V7X)),
                _const_block((LANES_V7X, GROUP)), _const_block((HALF, HALF))]

    fwd = pl.pallas_call(
        functools.partial(_mlstm_body, reverse=False, direction=0, nb=batch),
        in_specs=specs(False), out_specs=blk(0, False),
        out_shape=jax.ShapeDtypeStruct((batch, seq, GROUP), F32), name="mlstm_fwd", **common,
    )(z3, z3, z3, z3, z3, i_bias, f_bias, expand[0], bdh)
    return pl.pallas_call(
        functools.partial(_mlstm_body, reverse=True, direction=1, nb=batch),
        in_specs=specs(True) + [blk(ZD_O, True), blk(0, True), _const_block((1, GROUP))],
        out_specs=blk(0, True), out_shape=jax.ShapeDtypeStruct((batch, seq, GROUP), BF16),
        name="mlstm_bwd", **common,
    )(z3, z3, z3, z3, z3, i_bias, f_bias, expand[1], bdh, z3, fwd, norm_w)


def _trace_all(gen):
    try:
        while True:
            next(gen)
    except StopIteration as stop:
        return stop.value


def _trace_interleaved(main, side):
    live = [main] if side is None else [main, side]
    side_value = None
    while live:
        for gen in list(live):
            try:
                next(gen)
            except StopIteration as stop:
                live.remove(gen)
                if gen is side:
                    side_value = stop.value
    return side_value


def _rwkv_body(*refs, reverse, n_blocks, nb):
    if reverse:
        (z_ref, zp_ref, zn_ref, mu_ref, w0_ref, w2_ref, a0_ref, a2_ref, kk_ref, ka_ref, rk_ref,
         bdh_ref, g2_ref, lnw_ref, lnb_ref, wkvf_ref, bonf_ref, o_ref, s_ref) = refs
    else:
        (z_ref, zp_ref, zn_ref, mu_ref, w0_ref, w2_ref, a0_ref, a2_ref, kk_ref, ka_ref, rk_ref,
         bdh_ref, wkv_ref, bon_ref, s_ref) = refs
    step = pl.program_id(0)
    block = (n_blocks - 1 - step) if reverse else step

    @pl.when(step == 0)
    def _():
        s_ref[...] = jnp.zeros_like(s_ref)

    n = RWKV_CHUNK
    incl, strict = _order_masks(n, reverse)
    sub_masks = _sub_head_masks(BF16)
    units, chains = _units_and_chains(nb)

    def prepare(sub):
        return _rwkv_prepare(z_ref, zp_ref, zn_ref, sub, block, n_blocks, reverse, nb, mu_ref, w0_ref, w2_ref,
                             a0_ref, a2_ref, kk_ref, ka_ref, rk_ref, bdh_ref)

    def solve(prep, sub):
        rows = slice(sub * n, (sub + 1) * n)
        operand = lambda name, b, half: prep[name][b][:, _lanes_of(half)]
        state = {u: s_ref[u[0], u[1]] for u in units}
        xo = {u: _mm_nt(operand("ar", *u), state[u]) for u in units}
        yield
        ar_m = {c: operand("ar", c[0], c[1]) * sub_masks[c[2]] for c in chains}
        v_m = {c: operand("v", c[0], c[1]) * sub_masks[c[2]] for c in chains}
        sb = {c: _mm_nt(ar_m[c], operand("b_t", c[0], c[1])) for c in chains}
        yield
        sk = {c: _mm_nt(ar_m[c], operand("k_t", c[0], c[1])) for c in chains}
        yield
        lak_v = {c: _mm(jnp.where(strict, sk[c][0:n], 0.0), v_m[c]) for c in chains}
        x = {u: xo[u][0:n] + lak_v[u + (0,)] + lak_v[u + (1,)] for u in units}
        yield
        p = {c: jnp.where(strict, sb[c][0:n], 0.0).astype(BF16) for c in chains}
        u_c = {c: x[c[0:2]] * sub_masks[c[2]].astype(F32) for c in chains}
        u_c = {c: u_c[c] + _mm(p[c], u_c[c]) for c in chains}
        yield
        span = 1
        while 2 * span < n:
            p = {c: _mm(p[c], p[c]).astype(BF16) for c in chains}
            yield
            u_c = {c: u_c[c] + _mm(p[c], u_c[c]) for c in chains}
            yield
            span *= 2
        o_u = {c: _mm(jnp.where(incl, sb[c][n:2 * n], 0.0), u_c[c]) for c in chains}
        yield
        o_v = {c: _mm(jnp.where(incl, sk[c][n:2 * n], 0.0), v_m[c]) for c in chains}
        yield
        for u in units:
            b, half = u
            u_all = u_c[u + (0,)] + u_c[u + (1,)]
            upd = (_mm_tn(u_all, operand("b_t", b, half))
                   + _mm_tn(operand("v", b, half), operand("k_t", b, half)))
            s_ref[b, half] = prep["gam_end"][b][:, _lanes_of(half)] * (state[u] + bdh_ref[...] * upd)
        yield
        wkv_u = {u: xo[u][n:2 * n] + o_u[u + (0,)] + o_u[u + (1,)] + o_v[u + (0,)] + o_v[u + (1,)]
                 for u in units}
        if reverse:
            bdh = bdh_ref[...]
            wkv = jnp.concatenate([wkvf_ref[b, rows, :] + _join_halves(wkv_u, b) for b in range(nb)], axis=0)
            cen = wkv - _head_sum(wkv, bdh) * (1.0 / HEAD_DIM)
            var = _head_sum(cen * cen, bdh) * (1.0 / HEAD_DIM)
            gate = _mm(_sigmoid(prep["lat2"]), g2_ref[...])
            out = (cen * lax.rsqrt(var + RWKV_GN_EPS) * lnw_ref[...] + lnb_ref[...] + prep["bonus"]) * gate
            for b in range(nb):
                o_ref[b, rows, :] = (_rows_of(out, b, n)
                                     + bonf_ref[b, rows, :] * _rows_of(gate, b, n)).astype(o_ref.dtype)
        else:
            for b in range(nb):
                wkv_ref[b, rows, :] = _join_halves(wkv_u, b)
                bon_ref[b, rows, :] = _rows_of(prep["bonus"], b, n)

    order = list(range(RWKV_SUBCHUNKS))
    if reverse:
        order.reverse()
    prep = _trace_all(prepare(order[0]))
    for i, sub in enumerate(order):
        following = prepare(order[i + 1]) if i + 1 < len(order) else None
        prep = _trace_interleaved(solve(prep, sub), following)


def _rwkv_prepare(z_ref, zp_ref, zn_ref, sub, block, n_blocks, reverse, nb, mu_ref, w0_ref, w2_ref, a0_ref,
                  a2_ref, kk_ref, ka_ref, rk_ref, bdh_ref):
    n = RWKV_CHUNK
    lo, hi = sub * n, (sub + 1) * n
    ridx = lax.broadcasted_iota(jnp.int32, (n, 1), 0)
    shifted = []
    for b in range(nb):
        z = z_ref[b, lo:hi, :]
        if sub == 0:
            z_prev = jnp.where(block > 0, zp_ref[b, SUBLANES_V7X - 1:SUBLANES_V7X, :], 0.0)
        else:
            z_prev = z_ref[b, lo - 1:lo, :]
        if sub == RWKV_SUBCHUNKS - 1:
            z_next = jnp.where(block < n_blocks - 1, zn_ref[b, 0:1, :], 0.0)
        else:
            z_next = z_ref[b, hi:hi + 1, :]
        z_m1 = jnp.where(ridx == 0, z_prev, pltpu.roll(z, 1, 0))
        z_p1 = jnp.where(ridx == n - 1, z_next, pltpu.roll(z, n - 1, 0))
        shifted.append(z + mu_ref[...] * (0.5 * (z_m1 + z_p1) - z))
    zs = jnp.concatenate(shifted, axis=0)
    bdh = bdh_ref[...]
    r = zs[:, 0:GROUP]
    k = zs[:, GROUP:2 * GROUP]
    v = zs[:, 2 * GROUP:3 * GROUP]
    lat1 = zs[:, 3 * GROUP:3 * GROUP + LANES_V7X]
    lat2 = zs[:, 3 * GROUP + LANES_V7X:4 * GROUP]
    yield
    w_pre = w0_ref[...] + _mm(jnp.tanh(lat1), w2_ref[...])
    log_decay = -jnp.exp(-_softplus(-w_pre) - 0.5)
    a = _sigmoid(a0_ref[...] + _mm(lat1, a2_ref[...]))
    kk_raw = k * kk_ref[...]
    yield
    kk = kk_raw / jnp.maximum(jnp.sqrt(_head_sum(kk_raw * kk_raw, bdh)), 1e-12)
    kd = k * (1.0 + (a - 1.0) * ka_ref[...])
    yield
    bonus = _head_sum(r * kd * rk_ref[...], bdh) * v
    yield
    rows = nb * n
    row = lax.broadcasted_iota(jnp.int32, (rows, rows), 0)
    col = lax.broadcasted_iota(jnp.int32, (rows, rows), 1)
    same_seq = _div_pow2(row, n) == _div_pow2(col, n)
    before = (col >= row) if reverse else (col <= row)
    cum_incl = _sel_mm(same_seq & before, log_decay, terms=2)
    cum_excl = cum_incl - log_decay
    last = 0 if reverse else n - 1
    a_t = (-kk * jnp.exp(cum_excl)).astype(BF16)
    r_t = (r * jnp.exp(cum_incl)).astype(BF16)
    inv_g = jnp.exp(-cum_incl)
    b_t = (kk * a * inv_g).astype(BF16)
    k_t = (kd * inv_g).astype(BF16)
    vb = v.astype(BF16)
    per_seq = lambda t: [_rows_of(t, b, n) for b in range(nb)]
    return dict(ar=[jnp.concatenate([_rows_of(a_t, b, n), _rows_of(r_t, b, n)], axis=0) for b in range(nb)],
                b_t=per_seq(b_t), k_t=per_seq(k_t), v=per_seq(vb),
                gam_end=[jnp.exp(cum_incl[b * n + last:b * n + last + 1, :]) for b in range(nb)],
                bonus=bonus, lat2=lat2)


def _rwkv(z3, p, bdh):
    batch, seq, _ = z3.shape
    rows = RWKV_CHUNK * RWKV_SUBCHUNKS
    n_blocks = seq // rows
    rows8 = rows // SUBLANES_V7X
    last8 = seq // SUBLANES_V7X - 1
    wide = 4 * GROUP

    def halo(rev, offset):
        bidx = _chunk_index(n_blocks, rev)
        return pl.BlockSpec((batch, SUBLANES_V7X, wide), lambda c: (
            0, jnp.clip((bidx(c) + offset) * rows8 - (1 - offset), 0, last8), ZC_BLOCK_1024))

    vec = _const_block((1, GROUP))
    lora = _const_block((LANES_V7X, GROUP))
    out_blk = lambda rev: _seq_block(batch, rows, GROUP, n_blocks, rev, 0)
    out_shape = jax.ShapeDtypeStruct((batch, seq, GROUP), F32)

    def specs(rev):
        return [_seq_block(batch, rows, wide, n_blocks, rev, ZC_BLOCK_1024), halo(rev, 0), halo(rev, 1),
                _const_block((1, wide)), vec, lora, vec, lora, vec, vec, vec, _const_block((HALF, HALF))]

    common = dict(grid=(n_blocks,), scratch_shapes=[pltpu.VMEM((batch, N_HALVES, HALF, HALF), F32)],
                  compiler_params=_params("arbitrary"))
    wkv_f, bon_f = pl.pallas_call(
        functools.partial(_rwkv_body, reverse=False, n_blocks=n_blocks, nb=batch),
        in_specs=specs(False), out_specs=[out_blk(False), out_blk(False)],
        out_shape=[out_shape, out_shape], name="rwkv_fwd", **common,
    )(z3, z3, z3, p["mu"], p["w0"][0], p["w2"][0], p["a0"][0], p["a2"][0], p["k_k"], p["k_a"], p["r_k"], bdh)
    return pl.pallas_call(
        functools.partial(_rwkv_body, reverse=True, n_blocks=n_blocks, nb=batch),
        in_specs=specs(True) + [lora, vec, vec, out_blk(True), out_blk(True)],
        out_specs=out_blk(True), out_shape=jax.ShapeDtypeStruct((batch, seq, GROUP), BF16),
        name="rwkv_bwd", **common,
    )(z3, z3, z3, p["mu"], p["w0"][1], p["w2"][1], p["a0"][1], p["a2"][1], p["k_k"], p["k_a"], p["r_k"], bdh,
      p["g2"], p["ln_w"], p["ln_b"], wkv_f, bon_f)


def _outproj_body(x_ref, cb_ref, cc_ref, ch_ref, ccp_ref, chp_ref, ccn_ref, chn_ref, yb_ref, yc_ref, yd_ref,
                  cw_ref, cbias_ref, w_ref, g_ref, o_ref, *, tiles_per_seq):
    tm = x_ref.shape[0]
    t = pl.program_id(0) % tiles_per_seq
    u = cc_ref[...] * ch_ref[...]
    last8 = SUBLANES_V7X - 1
    u_prev = jnp.where(t > 0, ccp_ref[last8:last8 + 1, :] * chp_ref[last8:last8 + 1, :], 0.0)
    u_next = jnp.where(t < tiles_per_seq - 1, ccn_ref[0:1, :] * chn_ref[0:1, :], 0.0)
    ridx = lax.broadcasted_iota(jnp.int32, (tm, 1), 0)
    u_m1 = jnp.where(ridx == 0, u_prev, pltpu.roll(u, 1, 0))
    u_p1 = jnp.where(ridx == tm - 1, u_next, pltpu.roll(u, tm - 1, 0))
    cw = cw_ref[...]
    conv = cw[0:1, :] * u_m1 + cw[1:2, :] * u + cw[2:3, :] * u_p1 + cbias_ref[...]
    y_conv = cb_ref[...] * conv
    y = (_mm(y_conv, w_ref[0:GROUP, :]) + _mm(yb_ref[...], w_ref[GROUP:2 * GROUP, :])
         + _mm(yc_ref[...], w_ref[2 * GROUP:3 * GROUP, :]) + _mm(yd_ref[...], w_ref[3 * GROUP:4 * GROUP, :]))
    o_ref[...] = x_ref[...] + _rms_norm(y, g_ref[...])


def _outproj(x, z, y_ret, y_rwkv, y_mlstm, conv_w8, conv_b, w_out, g, seq):
    n = x.shape[0]
    tm = min(ROW_TILE, seq)
    tiles_per_seq = seq // tm
    rows8 = tm // SUBLANES_V7X
    last8 = n // SUBLANES_V7X - 1
    zblk = lambda col: pl.BlockSpec((tm, GROUP), lambda i: (i, col))
    prev = lambda col: pl.BlockSpec((SUBLANES_V7X, GROUP), lambda i: (jnp.maximum(i * rows8 - 1, 0), col))
    nxt = lambda col: pl.BlockSpec((SUBLANES_V7X, GROUP), lambda i: (jnp.minimum((i + 1) * rows8, last8), col))
    yblk = pl.BlockSpec((tm, GROUP), lambda i: (i, 0))
    const = lambda i: (0, 0)
    return pl.pallas_call(
        functools.partial(_outproj_body, tiles_per_seq=tiles_per_seq),
        grid=(n // tm,),
        in_specs=[pl.BlockSpec((tm, D_MODEL), lambda i: (i, 0)),
                  zblk(ZA_B), zblk(ZA_C), zblk(ZA_H), prev(ZA_C), prev(ZA_H), nxt(ZA_C), nxt(ZA_H),
                  yblk, yblk, yblk,
                  pl.BlockSpec((SUBLANES_V7X, GROUP), const), pl.BlockSpec((1, GROUP), const),
                  pl.BlockSpec((D_MODEL, D_MODEL), const), pl.BlockSpec((1, D_MODEL), const)],
        out_specs=pl.BlockSpec((tm, D_MODEL), lambda i: (i, 0)),
        out_shape=jax.ShapeDtypeStruct((n, D_MODEL), F32),
        compiler_params=_params("parallel"),
        name="outproj",
    )(x, z, z, z, z, z, z, z, y_ret, y_rwkv, y_mlstm, conv_w8, conv_b, w_out, g)


def _relayout_w_in(w):
    pad = lambda c: jnp.zeros((D_MODEL, c), w.dtype)
    a, b = w[:, 0:768], w[:, 768:1792]
    c = w[:, 1792:1792 + RWKV_COLS]
    d0 = 1792 + RWKV_COLS
    d = w[:, d0:d0 + 4 * GROUP]
    gi = w[:, d0 + 4 * GROUP:d0 + 4 * GROUP + 2 * N_HEADS]
    gf = w[:, d0 + 4 * GROUP + 2 * N_HEADS:d0 + 4 * GROUP + 4 * N_HEADS]
    gpad = LANES_V7X - 2 * N_HEADS
    return jnp.concatenate([b, c, pad(4 * GROUP - RWKV_COLS), a, d, gi, pad(gpad), gf, pad(gpad)],
                           axis=1).astype(BF16)


def _block_diag_ones():
    idx = np.arange(HALF) // HEAD_DIM
    return jnp.asarray((idx[:, None] == idx[None, :]).astype(np.float32))


def _gate_expand():
    j = np.arange(LANES_V7X)[:, None]
    head = np.arange(GROUP)[None, :] // HEAD_DIM
    return jnp.asarray(np.stack([(j == d * N_HEADS + head) for d in range(2)]).astype(np.float32))


def _rope_tables(seq):
    inv = 10000.0 ** (-jnp.arange(0, HEAD_DIM, 2, dtype=F32) / HEAD_DIM)
    ang = jnp.arange(seq, dtype=F32)[:, None] * inv[None, :]
    cos, sin = jnp.cos(ang), jnp.sin(ang)
    cos_t = jnp.tile(jnp.concatenate([cos, cos], axis=1), (1, N_HEADS))
    sin_t = jnp.tile(jnp.concatenate([-sin, sin], axis=1), (1, N_HEADS))
    return cos_t, sin_t


def _lane_pad(v, width):
    v = v.reshape(1, -1)
    return jnp.pad(v, ((0, 0), (0, width - v.shape[1])))


def _rwkv_params(mu, w0, w2, a0, a2, g2, k_k, k_a, r_k, ln_w, ln_b):
    rank_w, rank_a, rank_g = w2.shape[1], a2.shape[1], g2.shape[0]
    w2p = jnp.zeros((2, LANES_V7X, GROUP), F32)
    a2p = jnp.zeros((2, LANES_V7X, GROUP), F32)
    for d in range(2):
        w2p = w2p.at[d, d * rank_w:(d + 1) * rank_w].set(w2[d])
        a2p = a2p.at[d, 2 * rank_w + d * rank_a:2 * rank_w + (d + 1) * rank_a].set(a2[d])
    g2p = jnp.zeros((LANES_V7X, GROUP), F32).at[0:rank_g].set(g2)
    return dict(mu=_lane_pad(mu, 4 * GROUP), w0=w0.reshape(2, 1, GROUP), w2=w2p.astype(BF16),
                a0=a0.reshape(2, 1, GROUP), a2=a2p.astype(BF16), g2=g2p.astype(BF16),
                k_k=k_k.reshape(1, GROUP), k_a=k_a.reshape(1, GROUP), r_k=r_k.reshape(1, GROUP),
                ln_w=ln_w.reshape(1, GROUP), ln_b=ln_b.reshape(1, GROUP))


def kernel(x, norm_g, ffn_w_gate, ffn_w_up, ffn_w_down, w_in, w_out, conv_w, conv_b, ret_decay_logit, rwkv_mu, rwkv_w0, rwkv_w2, rwkv_a0, rwkv_a2, rwkv_g2, rwkv_k_k, rwkv_k_a, rwkv_r_k, rwkv_ln_w, rwkv_ln_b, mlstm_i_bias, mlstm_f_bias, mlstm_norm_w):
    batch, seq, _ = x.shape
    depth = norm_g.shape[0]
    n_tok = batch * seq
    assert seq % ROW_TILE == 0 and seq % RET_CHUNK == 0 and seq % (RWKV_CHUNK * RWKV_SUBCHUNKS) == 0
    cos_t, sin_t = _rope_tables(seq)
    bdh = _block_diag_ones()
    expand = _gate_expand()
    xf = x.reshape(n_tok, D_MODEL)
    for l in range(depth):
        g = norm_g[l].reshape(6, 1, D_MODEL)
        xf = _ffn(xf, g[0], g[1], ffn_w_gate[l, 0].astype(BF16), ffn_w_up[l, 0].astype(BF16),
                  ffn_w_down[l, 0].astype(BF16))
        z = _inproj(xf, g[2], _relayout_w_in(w_in[l]))
        z3 = z.reshape(batch, seq, Z_COLS)
        log_gamma = jax.nn.log_sigmoid(ret_decay_logit[l].astype(F32))
        y_ret = _retention(z3, cos_t, sin_t, log_gamma, bdh)
        y_rwkv = _rwkv(z3, _rwkv_params(
            rwkv_mu[l], rwkv_w0[l], rwkv_w2[l], rwkv_a0[l], rwkv_a2[l], rwkv_g2[l], rwkv_k_k[l],
            rwkv_k_a[l], rwkv_r_k[l], rwkv_ln_w[l], rwkv_ln_b[l]), bdh)
        y_mlstm = _mlstm(z3, _lane_pad(mlstm_i_bias[l], LANES_V7X), _lane_pad(mlstm_f_bias[l], LANES_V7X),
                         mlstm_norm_w[l].reshape(1, GROUP), expand, bdh)
        conv_w8 = jnp.pad(conv_w[l], ((0, SUBLANES_V7X - conv_w.shape[1]), (0, 0)))
        xf = _outproj(xf, z, y_ret.reshape(n_tok, GROUP), y_rwkv.reshape(n_tok, GROUP),
                      y_mlstm.reshape(n_tok, GROUP), conv_w8, conv_b[l].reshape(1, GROUP),
                      w_out[l].astype(BF16), g[3], seq)
        xf = _ffn(xf, g[4], g[5], ffn_w_gate[l, 1].astype(BF16), ffn_w_up[l, 1].astype(BF16),
                  ffn_w_down[l, 1].astype(BF16))
    return xf.reshape(batch, seq, D_MODEL)
```

```python
import functools

import numpy as np
import jax
import jax.numpy as jnp
from jax import lax
from jax.experimental import pallas as pl
from jax.experimental.pallas import tpu as pltpu

F32 = jnp.float32
BF16 = jnp.bfloat16

D_MODEL = 1024
GROUP = 256
N_HEADS = 4
HEAD_DIM = 64
D_FF = 2816
RWKV_COLS = 960
RMS_EPS = 1e-6
RWKV_GN_EPS = 64e-5

LANES_V7X = 128
SUBLANES_V7X = 8
VMEM_LIMIT_BYTES = 56 * 1024 * 1024

ROW_TILE = 512
FFN_CHUNK = 256
RET_CHUNK = 128
RET_SUBCHUNKS = 2
RWKV_CHUNK = 64
RWKV_SUBCHUNKS = 8
RWKV_PREP_SPACING = 3

HALF = LANES_V7X
N_HALVES = GROUP // HALF
HEADS_PER_HALF = HALF // HEAD_DIM

Z_COLS = 4096
ZB_Q, ZB_K, ZB_V, ZB_G = 0, 1, 2, 3
ZC_BLOCK_1024 = 1
ZA_B, ZA_C, ZA_H = 8, 9, 10
ZD_Q, ZD_K, ZD_V, ZD_O = 11, 12, 13, 14
ZD_GI_128, ZD_GF_128 = 30, 31


def _mm(a, b):
    return jnp.dot(a.astype(BF16), b.astype(BF16), preferred_element_type=F32)


def _mm_nt(a, b):
    return lax.dot_general(a.astype(BF16), b.astype(BF16), (((1,), (1,)), ((), ())),
                           preferred_element_type=F32)


def _mm_tn(a, b):
    return lax.dot_general(a.astype(BF16), b.astype(BF16), (((0,), (0,)), ((), ())),
                           preferred_element_type=F32)


def _bf16_terms(x, terms):
    out = []
    for _ in range(terms - 1):
        part = x.astype(BF16)
        out.append(part)
        x = x - part.astype(F32)
    out.append(x.astype(BF16))
    return out


def _sel_mm(sel, x, terms=3):
    s = sel.astype(BF16)
    return sum(jnp.dot(s, part, preferred_element_type=F32) for part in _bf16_terms(x, terms))


def _mm_sel(x, sel, terms=3):
    s = sel.astype(BF16)
    return sum(jnp.dot(part, s, preferred_element_type=F32) for part in _bf16_terms(x, terms))


def _head_sum(x, bdh):
    return jnp.concatenate([_mm(x[:, _lanes_of(h)], bdh) for h in range(N_HALVES)], axis=1)


def _rows_of(stacked, b, n):
    return stacked[b * n:(b + 1) * n]


def _div_pow2(x, d):
    shift = d.bit_length() - 1
    assert 1 << shift == d
    return lax.shift_right_logical(x, shift)


def _rms_norm(x, g):
    return x * lax.rsqrt(jnp.mean(x * x, axis=-1, keepdims=True) + RMS_EPS) * g


def _softplus(x):
    return jnp.maximum(x, 0.0) + jnp.log(1.0 + jnp.exp(-jnp.abs(x)))


def _sigmoid(x):
    return 1.0 / (1.0 + jnp.exp(-x))


def _order_masks(n, reverse, copies=1):
    row = lax.broadcasted_iota(jnp.int32, (n, copies * n), 0)
    col = lax.broadcasted_iota(jnp.int32, (n, copies * n), 1)
    if copies > 1:
        assert n & (n - 1) == 0
        col = col & (n - 1)
    if reverse:
        return col >= row, col > row
    return col <= row, col < row


def _sub_head_masks(dtype):
    lane = lax.broadcasted_iota(jnp.int32, (1, HALF), 1)
    return [((lane >= j * HEAD_DIM) & (lane < (j + 1) * HEAD_DIM)).astype(dtype)
            for j in range(HEADS_PER_HALF)]


def _lanes_of(half):
    return slice(half * HALF, (half + 1) * HALF)


def _units_and_chains(nb):
    units = [(b, half) for b in range(nb) for half in range(N_HALVES)]
    chains = [(b, half, j) for (b, half) in units for j in range(HEADS_PER_HALF)]
    return units, chains


def _join_halves(per_unit, b):
    return jnp.concatenate([per_unit[(b, half)] for half in range(N_HALVES)], axis=1)


def _params(*sem):
    return pltpu.CompilerParams(dimension_semantics=sem, vmem_limit_bytes=VMEM_LIMIT_BYTES)


def _ffn_body(x_ref, gin_ref, gout_ref, wg_ref, wu_ref, wd_ref, o_ref, acc_ref):
    x = x_ref[...]
    h = _rms_norm(x, gin_ref[...]).astype(BF16)
    for c in range(D_FF // FFN_CHUNK):
        lo = c * FFN_CHUNK
        gate = jnp.dot(h, wg_ref[:, lo:lo + FFN_CHUNK], preferred_element_type=F32)
        up = jnp.dot(h, wu_ref[:, lo:lo + FFN_CHUNK], preferred_element_type=F32)
        act = (gate * _sigmoid(gate) * up).astype(BF16)
        part = jnp.dot(act, wd_ref[lo:lo + FFN_CHUNK, :], preferred_element_type=F32)
        if c == 0:
            acc_ref[...] = part
        else:
            acc_ref[...] += part
    o_ref[...] = x + 0.5 * _rms_norm(acc_ref[...], gout_ref[...])


def _ffn(x, g_in, g_out, weights, layer, which):
    w_gate, w_up, w_down = weights
    n = x.shape[0]
    tm = min(ROW_TILE, n)
    const = lambda i: (0, 0)
    pick = lambda i: (layer, which, 0, 0)
    return pl.pallas_call(
        _ffn_body,
        grid=(n // tm,),
        in_specs=[
            pl.BlockSpec((tm, D_MODEL), lambda i: (i, 0)),
            pl.BlockSpec((1, D_MODEL), const),
            pl.BlockSpec((1, D_MODEL), const),
            pl.BlockSpec((None, None, D_MODEL, D_FF), pick),
            pl.BlockSpec((None, None, D_MODEL, D_FF), pick),
            pl.BlockSpec((None, None, D_FF, D_MODEL), pick),
        ],
        out_specs=pl.BlockSpec((tm, D_MODEL), lambda i: (i, 0)),
        out_shape=jax.ShapeDtypeStruct((n, D_MODEL), F32),
        scratch_shapes=[pltpu.VMEM((tm, D_MODEL), F32)],
        compiler_params=_params("parallel"),
        name="ffn",
    )(x, g_in, g_out, w_gate, w_up, w_down)


def _inproj_body(x_ref, g_ref, w_ref, z_ref):
    h = _rms_norm(x_ref[...], g_ref[...]).astype(BF16)
    z_ref[...] = jnp.dot(h, w_ref[...], preferred_element_type=F32)


def _inproj(x, g, w):
    n = x.shape[0]
    tm = min(ROW_TILE, n)
    return pl.pallas_call(
        _inproj_body,
        grid=(n // tm,),
        in_specs=[
            pl.BlockSpec((tm, D_MODEL), lambda i: (i, 0)),
            pl.BlockSpec((1, D_MODEL), lambda i: (0, 0)),
            pl.BlockSpec((D_MODEL, Z_COLS), lambda i: (0, 0)),
        ],
        out_specs=pl.BlockSpec((tm, Z_COLS), lambda i: (i, 0)),
        out_shape=jax.ShapeDtypeStruct((n, Z_COLS), F32),
        compiler_params=_params("parallel"),
        name="inproj",
    )(x, g, w)


def _chunk_index(nc, reverse):
    if reverse:
        return lambda c: nc - 1 - c
    return lambda c: c


def _seq_block(batch, rows, width, nc, reverse, col):
    cidx = _chunk_index(nc, reverse)
    return pl.BlockSpec((batch, rows, width), lambda c: (0, cidx(c), col))


def _const_block(shape):
    return pl.BlockSpec(shape, lambda c: (0,) * len(shape))


def _ret_section(in_refs, o_ref, st_ref, rows, *, reverse, nb):
    if reverse:
        lgs_ref, q_ref, k_ref, v_ref, cos_ref, sin_ref, lgv_ref, bdh_ref, g_ref, fwd_ref = in_refs
    else:
        lgs_ref, q_ref, k_ref, v_ref, cos_ref, sin_ref, lgv_ref, bdh_ref = in_refs
    n = RET_CHUNK
    lane = lax.broadcasted_iota(jnp.int32, (1, GROUP), 1)
    first_half = (lane % HEAD_DIM) < (HEAD_DIM // 2)
    cos = cos_ref[rows, :]
    sin = sin_ref[rows, :]

    def rot(t):
        swapped = jnp.where(first_half, pltpu.roll(t, GROUP - HEAD_DIM // 2, 1),
                            pltpu.roll(t, HEAD_DIM // 2, 1))
        return t * cos + swapped * sin

    lg = lgv_ref[...]
    ti = lax.broadcasted_iota(jnp.int32, (n, 1), 0).astype(F32)
    if reverse:
        q_exp, k_exp = n - ti, ti
    else:
        q_exp, k_exp = ti + 1.0, n - 1.0 - ti
    q_dec = jnp.exp(q_exp * lg)
    k_dec = jnp.exp(k_exp * lg)
    c_dec = jnp.exp(float(n) * lg)
    row = lax.broadcasted_iota(jnp.int32, (n, n), 0)
    col = lax.broadcasted_iota(jnp.int32, (n, n), 1)
    if reverse:
        rel, msk = (col - row).astype(F32), col > row
    else:
        rel, msk = (row - col).astype(F32), col <= row
    intra = [jnp.where(msk, jnp.exp(jnp.where(msk, rel, 0.0) * lgs_ref[h]), 0.0) for h in range(N_HEADS)]
    sub_masks = _sub_head_masks(BF16)
    units, chains = _units_and_chains(nb)

    q = [(rot(q_ref[b, rows, :]) * HEAD_DIM ** -0.5).astype(BF16) for b in range(nb)]
    k_rot = [rot(k_ref[b, rows, :]) for b in range(nb)]
    k = [t.astype(BF16) for t in k_rot]
    k_scaled = [(t * k_dec).astype(BF16) for t in k_rot]
    v = [v_ref[b, rows, :].astype(BF16) for b in range(nb)]
    yield
    state = {u: st_ref[u[0], u[1]] for u in units}
    inter = {(b, h): _mm(q[b][:, _lanes_of(h)], state[(b, h)]) for (b, h) in units}
    yield
    s = {(b, h, j): _mm_nt(q[b][:, _lanes_of(h)] * sub_masks[j], k[b][:, _lanes_of(h)])
         for (b, h, j) in chains}
    yield
    p = {(b, h, j): (s[(b, h, j)] * intra[h * HEADS_PER_HALF + j]).astype(BF16) for (b, h, j) in chains}
    intra_out = {
        (b, h): _mm(jnp.concatenate([p[(b, h, j)] for j in range(HEADS_PER_HALF)], axis=1),
                    jnp.concatenate([v[b][:, _lanes_of(h)] * sub_masks[j] for j in range(HEADS_PER_HALF)],
                                    axis=0))
        for (b, h) in units}
    yield
    for (b, h) in units:
        sl = _lanes_of(h)
        st_ref[b, h] = c_dec[:, sl] * state[(b, h)] + bdh_ref[...] * _mm_tn(k_scaled[b][:, sl], v[b][:, sl])
    yield
    outs = [q_dec * _join_halves(inter, b) + _join_halves(intra_out, b) for b in range(nb)]
    if reverse:
        tot = jnp.concatenate([fwd_ref[b, rows, :] + outs[b] for b in range(nb)], axis=0)
        normed = tot * lax.rsqrt(_head_sum(tot * tot, bdh_ref[...]) * (1.0 / HEAD_DIM) + RMS_EPS)
        for b in range(nb):
            g = g_ref[b, rows, :]
            o_ref[b, rows, :] = (g * _sigmoid(g) * _rows_of(normed, b, n)).astype(o_ref.dtype)
    else:
        for b in range(nb):
            o_ref[b, rows, :] = outs[b]


def _ret_operands(z3, cos_t, sin_t, log_gamma, bdh, rev, fwd):
    batch, seq, _ = z3.shape
    n = RET_CHUNK * RET_SUBCHUNKS
    nc = seq // n
    blk = lambda col: _seq_block(batch, n, GROUP, nc, rev, col)
    cidx = _chunk_index(nc, rev)
    tab = pl.BlockSpec((n, GROUP), lambda c: (cidx(c), 0))
    d = int(rev)
    lgv = jnp.repeat(log_gamma, HEAD_DIM, axis=1)
    specs = [pl.BlockSpec(memory_space=pltpu.SMEM), blk(ZB_Q), blk(ZB_K), blk(ZB_V), tab, tab,
             _const_block((1, GROUP)), _const_block((HALF, HALF))]
    operands = [log_gamma[d], z3, z3, z3, cos_t, sin_t, lgv[d:d + 1], bdh]
    if rev:
        specs += [blk(ZB_G), blk(0)]
        operands += [z3, fwd]
    return specs, operands


def _mlstm_section(in_refs, o_ref, c_ref, n_ref, m_ref, rows, *, reverse, nb):
    if reverse:
        (q_ref, k_ref, v_ref, gi_ref, gf_ref, ib_ref, fb_ref, ed_ref, bdh_ref, o_gate_ref, fwd_ref,
         nw_ref) = in_refs
    else:
        q_ref, k_ref, v_ref, gi_ref, gf_ref, ib_ref, fb_ref, ed_ref, bdh_ref = in_refs
    direction = int(reverse)
    n = RET_CHUNK
    incl, _ = _order_masks(n, reverse)
    sub_bf = _sub_head_masks(BF16)
    sub_f32 = _sub_head_masks(F32)
    units, chains = _units_and_chains(nb)
    ed = ed_ref[...]
    last = 0 if reverse else n - 1

    qf = [q_ref[b, rows, :] * HEAD_DIM ** -0.5 for b in range(nb)]
    q = [t.astype(BF16) for t in qf]
    kf = [k_ref[b, rows, :] for b in range(nb)]
    k = [t.astype(BF16) for t in kf]
    v = [v_ref[b, rows, :].astype(BF16) for b in range(nb)]
    gi = [gi_ref[b, rows, :] + ib_ref[...] for b in range(nb)]
    lf = [-_softplus(-(gf_ref[b, rows, :] + fb_ref[...])) for b in range(nb)]
    yield
    bcum_lanes = _sel_mm(incl, jnp.concatenate(lf, axis=1))
    bcum = [bcum_lanes[:, b * LANES_V7X:(b + 1) * LANES_V7X] for b in range(nb)]
    yield
    row_terms = [(gi[b] - bcum[b]).T for b in range(nb)]
    li_all = _mm_sel(jnp.concatenate(gi, axis=0), ed)
    yield
    bcum_all = _mm_sel(jnp.concatenate(bcum, axis=0), ed)
    li_x = [_rows_of(li_all, b, n) for b in range(nb)]
    bcum_x = [_rows_of(bcum_all, b, n) for b in range(nb)]
    m_x = [m_ref[b] for b in range(nb)]
    n_x = [n_ref[b] for b in range(nb)]
    yield
    head_of = lambda c: c[1] * HEADS_PER_HALF + c[2]
    gate_lane = lambda c: direction * N_HEADS + head_of(c)
    colv = {c: bcum[c[0]][:, gate_lane(c):gate_lane(c) + 1] for c in chains}
    col_b = {c: jnp.broadcast_to(colv[c], (n, n)) for c in chains}
    yield
    d_log = {c: jnp.where(incl, col_b[c] + row_terms[c[0]][gate_lane(c):gate_lane(c) + 1, :], -jnp.inf)
             for c in chains}
    row_max = {c: jnp.max(d_log[c], axis=1, keepdims=True) for c in chains}
    yield
    mt = {c: jnp.maximum(colv[c] + m_x[c[0]][:, head_of(c) * HEAD_DIM:head_of(c) * HEAD_DIM + 1], row_max[c])
          for c in chains}
    mt_b = {c: jnp.broadcast_to(mt[c], (n, n)) for c in chains}
    yield
    dm = {c: jnp.exp(d_log[c] - mt_b[c]) for c in chains}
    cmat = {u: c_ref[u[0], u[1]] for u in units}
    s = {(b, h, j): (_mm_nt(q[b][:, _lanes_of(h)] * sub_bf[j], k[b][:, _lanes_of(h)]) * dm[(b, h, j)]
                     ).astype(BF16) for (b, h, j) in chains}
    yield
    ones_rows = _div_pow2(lax.broadcasted_iota(jnp.int32, (HEADS_PER_HALF * n, HALF), 0), n)
    ones_lanes = _div_pow2(lax.broadcasted_iota(jnp.int32, (HEADS_PER_HALF * n, HALF), 1), HEAD_DIM)
    head_ones = (ones_rows == ones_lanes).astype(BF16)
    num_u, den_u = {}, {}
    for (b, h) in units:
        v_h = v[b][:, _lanes_of(h)]
        rhs = jnp.concatenate([jnp.concatenate([v_h * sub_bf[j] for j in range(HEADS_PER_HALF)], axis=0),
                               head_ones], axis=1)
        nd = _mm(jnp.concatenate([s[(b, h, j)] for j in range(HEADS_PER_HALF)], axis=1), rhs)
        num_u[(b, h)] = nd[:, 0:HALF]
        den_u[(b, h)] = nd[:, HALF:2 * HALF]
    yield
    inter_q = {(b, h): _mm(q[b][:, _lanes_of(h)], cmat[(b, h)]) for (b, h) in units}
    q_dot_n = _head_sum(jnp.concatenate([qf[b] * n_x[b] for b in range(nb)], axis=0), bdh_ref[...])
    mt_u = {(b, h): mt[(b, h, 0)] * sub_f32[0] + mt[(b, h, 1)] * sub_f32[1] for (b, h) in units}
    yield

    outs = []
    for b in range(nb):
        mt_x = _join_halves(mt_u, b)
        inter_w = jnp.exp(bcum_x[b] + m_x[b] - mt_x)
        num = _join_halves(num_u, b) + inter_w * _join_halves(inter_q, b)
        den = _join_halves(den_u, b) + inter_w * _rows_of(q_dot_n, b, n)
        outs.append(num / jnp.maximum(jnp.abs(den), jnp.exp(-mt_x)))
        b_end = bcum_x[b][last:last + 1, :]
        w_log = b_end - bcum_x[b] + li_x[b]
        m_new = jnp.maximum(b_end + m_x[b], jnp.max(w_log, axis=0, keepdims=True))
        old_w = jnp.exp(b_end + m_x[b] - m_new)
        new_w = jnp.exp(w_log - m_new)
        k_w = (kf[b] * new_w).astype(BF16)
        for h in range(N_HALVES):
            sl = _lanes_of(h)
            c_ref[b, h] = old_w[:, sl] * cmat[(b, h)] + bdh_ref[...] * _mm_tn(k_w[:, sl], v[b][:, sl])
        n_ref[b] = old_w * n_x[b] + jnp.sum(new_w * kf[b], axis=0, keepdims=True)
        m_ref[b] = m_new
    if reverse:
        tot = jnp.concatenate([fwd_ref[b, rows, :] + outs[b] for b in range(nb)], axis=0)
        normed = tot * lax.rsqrt(_head_sum(tot * tot, bdh_ref[...]) * (1.0 / HEAD_DIM) + RMS_EPS)
        for b in range(nb):
            o_ref[b, rows, :] = (_sigmoid(o_gate_ref[b, rows, :]) * _rows_of(normed, b, n)
                                 * nw_ref[...]).astype(o_ref.dtype)
    else:
        for b in range(nb):
            o_ref[b, rows, :] = outs[b]


def _mlstm_operands(z3, i_bias, f_bias, norm_w, expand, bdh, rev, fwd):
    batch, seq, _ = z3.shape
    n = RET_CHUNK * RET_SUBCHUNKS
    nc = seq // n
    blk = lambda col: _seq_block(batch, n, GROUP, nc, rev, col)
    gblk = lambda col: _seq_block(batch, n, LANES_V7X, nc, rev, col)
    specs = [blk(ZD_Q), blk(ZD_K), blk(ZD_V), gblk(ZD_GI_128), gblk(ZD_GF_128),
             _const_block((1, LANES_V7X)), _const_block((1, LANES_V7X)),
             _const_block((LANES_V7X, GROUP)), _const_block((HALF, HALF))]
    operands = [z3, z3, z3, z3, z3, i_bias, f_bias, expand[int(rev)], bdh]
    if rev:
        specs += [blk(ZD_O), blk(0), _const_block((1, GROUP))]
        operands += [z3, fwd, norm_w]
    return specs, operands


def _in_sequence(sections):
    for section in sections:
        yield from section


def _ret_mlstm_body(*refs, reverse, nb, n_ret_in):
    n_in = len(refs) - 6
    ret_o, mlstm_o, st_ref, c_ref, n_ref, m_ref = refs[n_in:]

    @pl.when(pl.program_id(0) == 0)
    def _():
        for ref in (st_ref, c_ref, n_ref, m_ref):
            ref[...] = jnp.zeros_like(ref)

    order = list(range(RET_SUBCHUNKS))
    if reverse:
        order.reverse()
    rows = lambda sub: slice(sub * RET_CHUNK, (sub + 1) * RET_CHUNK)
    _trace_interleaved(
        _in_sequence([_mlstm_section(refs[n_ret_in:n_in], mlstm_o, c_ref, n_ref, m_ref, rows(sub),
                                     reverse=reverse, nb=nb) for sub in order]),
        _in_sequence([_ret_section(refs[:n_ret_in], ret_o, st_ref, rows(sub), reverse=reverse, nb=nb)
                      for sub in order]))


def _ret_mlstm(z3, cos_t, sin_t, log_gamma, i_bias, f_bias, norm_w, expand, bdh):
    batch, seq, _ = z3.shape
    n = RET_CHUNK * RET_SUBCHUNKS
    nc = seq // n
    scratch = [pltpu.VMEM((batch, N_HALVES, HALF, HALF), F32), pltpu.VMEM((batch, N_HALVES, HALF, HALF), F32),
               pltpu.VMEM((batch, 1, GROUP), F32), pltpu.VMEM((batch, 1, GROUP), F32)]
    partial_out = (None, None)
    for rev in (False, True):
        ret_specs, ret_ops = _ret_operands(z3, cos_t, sin_t, log_gamma, bdh, rev, partial_out[0])
        ml_specs, ml_ops = _mlstm_operands(z3, i_bias, f_bias, norm_w, expand, bdh, rev, partial_out[1])
        out_blk = _seq_block(batch, n, GROUP, nc, rev, 0)
        out_shape = jax.ShapeDtypeStruct((batch, seq, GROUP), BF16 if rev else F32)
        partial_out = pl.pallas_call(
            functools.partial(_ret_mlstm_body, reverse=rev, nb=batch, n_ret_in=len(ret_specs)),
            grid=(nc,), in_specs=ret_specs + ml_specs, out_specs=[out_blk, out_blk],
            out_shape=[out_shape, out_shape], scratch_shapes=scratch,
            compiler_params=_params("arbitrary"), name="ret_mlstm_bwd" if rev else "ret_mlstm_fwd",
        )(*ret_ops, *ml_ops)
    return partial_out


def _trace_all(gen):
    try:
        while True:
            next(gen)
    except StopIteration as stop:
        return stop.value


def _pause(stages):
    for _ in range(stages):
        yield


def _trace_interleaved(main, side):
    live = [main] if side is None else [main, side]
    side_value = None
    while live:
        for gen in list(live):
            try:
                next(gen)
            except StopIteration as stop:
                live.remove(gen)
                if gen is side:
                    side_value = stop.value
    return side_value


def _rwkv_body(*refs, reverse, n_blocks, nb):
    if reverse:
        (z_ref, zp_ref, zn_ref, mu_ref, w0_ref, w2_ref, a0_ref, a2_ref, kk_ref, ka_ref, rk_ref,
         bdh_ref, g2_ref, lnw_ref, lnb_ref, wkvf_ref, bonf_ref, o_ref, s_ref) = refs
    else:
        (z_ref, zp_ref, zn_ref, mu_ref, w0_ref, w2_ref, a0_ref, a2_ref, kk_ref, ka_ref, rk_ref,
         bdh_ref, wkv_ref, bon_ref, s_ref) = refs
    step = pl.program_id(0)
    block = (n_blocks - 1 - step) if reverse else step

    @pl.when(step == 0)
    def _():
        s_ref[...] = jnp.zeros_like(s_ref)

    n = RWKV_CHUNK
    incl2, strict2 = _order_masks(n, reverse, copies=2)
    sub_masks = _sub_head_masks(BF16)
    units, chains = _units_and_chains(nb)

    def prepare(sub):
        return _rwkv_prepare(z_ref, zp_ref, zn_ref, sub, block, n_blocks, reverse, nb, mu_ref, w0_ref, w2_ref,
                             a0_ref, a2_ref, kk_ref, ka_ref, rk_ref, bdh_ref)

    def solve(prep, sub):
        rows = slice(sub * n, (sub + 1) * n)
        operand = lambda name, b, half: prep[name][b][:, _lanes_of(half)]
        state = {u: s_ref[u[0], u[1]] for u in units}
        xo = {u: _mm_nt(operand("ar", *u), state[u]) for u in units}
        yield
        ar_m = {c: operand("ar", c[0], c[1]) * sub_masks[c[2]] for c in chains}
        v_m = {c: operand("v", c[0], c[1]) * sub_masks[c[2]] for c in chains}
        bk = {u: jnp.concatenate([operand("b_t", *u), operand("k_t", *u)], axis=0) for u in units}
        scores = {c: _mm_nt(ar_m[c], bk[c[0:2]]) for c in chains}
        yield
        low = {c: jnp.where(strict2, scores[c][0:n], 0.0).astype(BF16) for c in chains}
        out_w = {c: jnp.where(incl2, scores[c][n:2 * n], 0.0).astype(BF16) for c in chains}
        zeros_n = jnp.zeros((n, HALF), BF16)
        lak_v = {c: _mm(low[c], jnp.concatenate([zeros_n, v_m[c]], axis=0)) for c in chains}
        x = {u: xo[u][0:n] + lak_v[u + (0,)] + lak_v[u + (1,)] for u in units}
        yield
        p = {c: low[c][:, 0:n] for c in chains}
        u_c = {c: x[c[0:2]] * sub_masks[c[2]].astype(F32) for c in chains}
        u_c = {c: u_c[c] + _mm(p[c], u_c[c]) for c in chains}
        yield
        span = 1
        while 2 * span < n:
            p = {c: _mm(p[c], p[c]).astype(BF16) for c in chains}
            yield
            u_c = {c: u_c[c] + _mm(p[c], u_c[c]) for c in chains}
            yield
            span *= 2
        o_c = {c: _mm(out_w[c], jnp.concatenate([u_c[c].astype(BF16), v_m[c]], axis=0)) for c in chains}
        yield
        for u in units:
            b, half = u
            u_all = u_c[u + (0,)] + u_c[u + (1,)]
            upd = (_mm_tn(u_all, operand("b_t", b, half))
                   + _mm_tn(operand("v", b, half), operand("k_t", b, half)))
            s_ref[b, half] = prep["gam_end"][b][:, _lanes_of(half)] * (state[u] + bdh_ref[...] * upd)
        yield
        wkv_u = {u: xo[u][n:2 * n] + o_c[u + (0,)] + o_c[u + (1,)] for u in units}
        if reverse:
            bdh = bdh_ref[...]
            wkv = jnp.concatenate([wkvf_ref[b, rows, :] + _join_halves(wkv_u, b) for b in range(nb)], axis=0)
            cen = wkv - _head_sum(wkv, bdh) * (1.0 / HEAD_DIM)
            var = _head_sum(cen * cen, bdh) * (1.0 / HEAD_DIM)
            gate = _mm(_sigmoid(prep["lat2"]), g2_ref[...])
            out = (cen * lax.rsqrt(var + RWKV_GN_EPS) * lnw_ref[...] + lnb_ref[...] + prep["bonus"]) * gate
            for b in range(nb):
                o_ref[b, rows, :] = (_rows_of(out, b, n)
                                     + bonf_ref[b, rows, :] * _rows_of(gate, b, n)).astype(o_ref.dtype)
        else:
            for b in range(nb):
                wkv_ref[b, rows, :] = _join_halves(wkv_u, b)
                bon_ref[b, rows, :] = _rows_of(prep["bonus"], b, n)

    order = list(range(RWKV_SUBCHUNKS))
    if reverse:
        order.reverse()
    prep = _trace_all(prepare(order[0]))
    for i, sub in enumerate(order):
        following = prepare(order[i + 1]) if i + 1 < len(order) else None
        prep = _trace_interleaved(solve(prep, sub), following)


def _rwkv_prepare(z_ref, zp_ref, zn_ref, sub, block, n_blocks, reverse, nb, mu_ref, w0_ref, w2_ref, a0_ref,
                  a2_ref, kk_ref, ka_ref, rk_ref, bdh_ref):
    n = RWKV_CHUNK
    lo, hi = sub * n, (sub + 1) * n
    ridx = lax.broadcasted_iota(jnp.int32, (n, 1), 0)
    shifted = []
    for b in range(nb):
        z = z_ref[b, lo:hi, :]
        if sub == 0:
            z_prev = jnp.where(block > 0, zp_ref[b, SUBLANES_V7X - 1:SUBLANES_V7X, :], 0.0)
        else:
            z_prev = z_ref[b, lo - 1:lo, :]
        if sub == RWKV_SUBCHUNKS - 1:
            z_next = jnp.where(block < n_blocks - 1, zn_ref[b, 0:1, :], 0.0)
        else:
            z_next = z_ref[b, hi:hi + 1, :]
        z_m1 = jnp.where(ridx == 0, z_prev, pltpu.roll(z, 1, 0))
        z_p1 = jnp.where(ridx == n - 1, z_next, pltpu.roll(z, n - 1, 0))
        shifted.append(z + mu_ref[...] * (0.5 * (z_m1 + z_p1) - z))
    zs = jnp.concatenate(shifted, axis=0)
    bdh = bdh_ref[...]
    r = zs[:, 0:GROUP]
    k = zs[:, GROUP:2 * GROUP]
    v = zs[:, 2 * GROUP:3 * GROUP]
    lat1 = zs[:, 3 * GROUP:3 * GROUP + LANES_V7X]
    lat2 = zs[:, 3 * GROUP + LANES_V7X:4 * GROUP]
    yield from _pause(RWKV_PREP_SPACING)
    w_pre = w0_ref[...] + _mm(jnp.tanh(lat1), w2_ref[...])
    log_decay = -jnp.exp(-_softplus(-w_pre) - 0.5)
    a = _sigmoid(a0_ref[...] + _mm(lat1, a2_ref[...]))
    kk_raw = k * kk_ref[...]
    yield from _pause(RWKV_PREP_SPACING)
    kk = kk_raw / jnp.maximum(jnp.sqrt(_head_sum(kk_raw * kk_raw, bdh)), 1e-12)
    kd = k * (1.0 + (a - 1.0) * ka_ref[...])
    yield from _pause(RWKV_PREP_SPACING)
    bonus = _head_sum(r * kd * rk_ref[...], bdh) * v
    yield from _pause(RWKV_PREP_SPACING)
    rows = nb * n
    row = lax.broadcasted_iota(jnp.int32, (rows, rows), 0)
    col = lax.broadcasted_iota(jnp.int32, (rows, rows), 1)
    same_seq = _div_pow2(row, n) == _div_pow2(col, n)
    before = (col >= row) if reverse else (col <= row)
    cum_incl = _sel_mm(same_seq & before, log_decay, terms=2)
    cum_excl = cum_incl - log_decay
    last = 0 if reverse else n - 1
    a_t = (-kk * jnp.exp(cum_excl)).astype(BF16)
    r_t = (r * jnp.exp(cum_incl)).astype(BF16)
    inv_g = jnp.exp(-cum_incl)
    b_t = (kk * a * inv_g).astype(BF16)
    k_t = (kd * inv_g).astype(BF16)
    vb = v.astype(BF16)
    per_seq = lambda t: [_rows_of(t, b, n) for b in range(nb)]
    return dict(ar=[jnp.concatenate([_rows_of(a_t, b, n), _rows_of(r_t, b, n)], axis=0) for b in range(nb)],
                b_t=per_seq(b_t), k_t=per_seq(k_t), v=per_seq(vb),
                gam_end=[jnp.exp(cum_incl[b * n + last:b * n + last + 1, :]) for b in range(nb)],
                bonus=bonus, lat2=lat2)


def _rwkv(z3, p, bdh):
    batch, seq, _ = z3.shape
    rows = RWKV_CHUNK * RWKV_SUBCHUNKS
    n_blocks = seq // rows
    rows8 = rows // SUBLANES_V7X
    last8 = seq // SUBLANES_V7X - 1
    wide = 4 * GROUP

    def halo(rev, offset):
        bidx = _chunk_index(n_blocks, rev)
        return pl.BlockSpec((batch, SUBLANES_V7X, wide), lambda c: (
            0, jnp.clip((bidx(c) + offset) * rows8 - (1 - offset), 0, last8), ZC_BLOCK_1024))

    vec = _const_block((1, GROUP))
    lora = _const_block((LANES_V7X, GROUP))
    out_blk = lambda rev: _seq_block(batch, rows, GROUP, n_blocks, rev, 0)
    out_shape = jax.ShapeDtypeStruct((batch, seq, GROUP), F32)

    def specs(rev):
        return [_seq_block(batch, rows, wide, n_blocks, rev, ZC_BLOCK_1024), halo(rev, 0), halo(rev, 1),
                _const_block((1, wide)), vec, lora, vec, lora, vec, vec, vec, _const_block((HALF, HALF))]

    common = dict(grid=(n_blocks,), scratch_shapes=[pltpu.VMEM((batch, N_HALVES, HALF, HALF), F32)],
                  compiler_params=_params("arbitrary"))
    wkv_f, bon_f = pl.pallas_call(
        functools.partial(_rwkv_body, reverse=False, n_blocks=n_blocks, nb=batch),
        in_specs=specs(False), out_specs=[out_blk(False), out_blk(False)],
        out_shape=[out_shape, out_shape], name="rwkv_fwd", **common,
    )(z3, z3, z3, p["mu"], p["w0"][0], p["w2"][0], p["a0"][0], p["a2"][0], p["k_k"], p["k_a"], p["r_k"], bdh)
    return pl.pallas_call(
        functools.partial(_rwkv_body, reverse=True, n_blocks=n_blocks, nb=batch),
        in_specs=specs(True) + [lora, vec, vec, out_blk(True), out_blk(True)],
        out_specs=out_blk(True), out_shape=jax.ShapeDtypeStruct((batch, seq, GROUP), BF16),
        name="rwkv_bwd", **common,
    )(z3, z3, z3, p["mu"], p["w0"][1], p["w2"][1], p["a0"][1], p["a2"][1], p["k_k"], p["k_a"], p["r_k"], bdh,
      p["g2"], p["ln_w"], p["ln_b"], wkv_f, bon_f)


def _outproj_body(x_ref, cb_ref, cc_ref, ch_ref, ccp_ref, chp_ref, ccn_ref, chn_ref, yb_ref, yc_ref, yd_ref,
                  cw_ref, cbias_ref, w_ref, g_ref, o_ref, *, tiles_per_seq):
    tm = x_ref.shape[0]
    t = pl.program_id(0) % tiles_per_seq
    u = cc_ref[...] * ch_ref[...]
    last8 = SUBLANES_V7X - 1
    u_prev = jnp.where(t > 0, ccp_ref[last8:last8 + 1, :] * chp_ref[last8:last8 + 1, :], 0.0)
    u_next = jnp.where(t < tiles_per_seq - 1, ccn_ref[0:1, :] * chn_ref[0:1, :], 0.0)
    ridx = lax.broadcasted_iota(jnp.int32, (tm, 1), 0)
    u_m1 = jnp.where(ridx == 0, u_prev, pltpu.roll(u, 1, 0))
    u_p1 = jnp.where(ridx == tm - 1, u_next, pltpu.roll(u, tm - 1, 0))
    cw = cw_ref[...]
    conv = cw[0:1, :] * u_m1 + cw[1:2, :] * u + cw[2:3, :] * u_p1 + cbias_ref[...]
    y_conv = cb_ref[...] * conv
    y = (_mm(y_conv, w_ref[0:GROUP, :]) + _mm(yb_ref[...], w_ref[GROUP:2 * GROUP, :])
         + _mm(yc_ref[...], w_ref[2 * GROUP:3 * GROUP, :]) + _mm(yd_ref[...], w_ref[3 * GROUP:4 * GROUP, :]))
    o_ref[...] = x_ref[...] + _rms_norm(y, g_ref[...])


def _outproj(x, z, y_ret, y_rwkv, y_mlstm, conv_w8, conv_b, w_out, layer, g, seq):
    n = x.shape[0]
    tm = min(ROW_TILE, seq)
    tiles_per_seq = seq // tm
    rows8 = tm // SUBLANES_V7X
    last8 = n // SUBLANES_V7X - 1
    zblk = lambda col: pl.BlockSpec((tm, GROUP), lambda i: (i, col))
    prev = lambda col: pl.BlockSpec((SUBLANES_V7X, GROUP), lambda i: (jnp.maximum(i * rows8 - 1, 0), col))
    nxt = lambda col: pl.BlockSpec((SUBLANES_V7X, GROUP), lambda i: (jnp.minimum((i + 1) * rows8, last8), col))
    yblk = pl.BlockSpec((tm, GROUP), lambda i: (i, 0))
    const = lambda i: (0, 0)
    return pl.pallas_call(
        functools.partial(_outproj_body, tiles_per_seq=tiles_per_seq),
        grid=(n // tm,),
        in_specs=[pl.BlockSpec((tm, D_MODEL), lambda i: (i, 0)),
                  zblk(ZA_B), zblk(ZA_C), zblk(ZA_H), prev(ZA_C), prev(ZA_H), nxt(ZA_C), nxt(ZA_H),
                  yblk, yblk, yblk,
                  pl.BlockSpec((SUBLANES_V7X, GROUP), const), pl.BlockSpec((1, GROUP), const),
                  pl.BlockSpec((None, D_MODEL, D_MODEL), lambda i: (layer, 0, 0)),
                  pl.BlockSpec((1, D_MODEL), const)],
        out_specs=pl.BlockSpec((tm, D_MODEL), lambda i: (i, 0)),
        out_shape=jax.ShapeDtypeStruct((n, D_MODEL), F32),
        compiler_params=_params("parallel"),
        name="outproj",
    )(x, z, z, z, z, z, z, z, y_ret, y_rwkv, y_mlstm, conv_w8, conv_b, w_out, g)


def _relayout_w_in(w):
    pad = lambda c: jnp.zeros((D_MODEL, c), w.dtype)
    a, b = w[:, 0:768], w[:, 768:1792]
    c = w[:, 1792:1792 + RWKV_COLS]
    d0 = 1792 + RWKV_COLS
    d = w[:, d0:d0 + 4 * GROUP]
    gi = w[:, d0 + 4 * GROUP:d0 + 4 * GROUP + 2 * N_HEADS]
    gf = w[:, d0 + 4 * GROUP + 2 * N_HEADS:d0 + 4 * GROUP + 4 * N_HEADS]
    gpad = LANES_V7X - 2 * N_HEADS
    return jnp.concatenate([b, c, pad(4 * GROUP - RWKV_COLS), a, d, gi, pad(gpad), gf, pad(gpad)],
                           axis=1).astype(BF16)


def _block_diag_ones():
    idx = np.arange(HALF) // HEAD_DIM
    return jnp.asarray((idx[:, None] == idx[None, :]).astype(np.float32))


def _gate_expand():
    j = np.arange(LANES_V7X)[:, None]
    head = np.arange(GROUP)[None, :] // HEAD_DIM
    return jnp.asarray(np.stack([(j == d * N_HEADS + head) for d in range(2)]).astype(np.float32))


def _rope_tables(seq):
    inv = 10000.0 ** (-jnp.arange(0, HEAD_DIM, 2, dtype=F32) / HEAD_DIM)
    ang = jnp.arange(seq, dtype=F32)[:, None] * inv[None, :]
    cos, sin = jnp.cos(ang), jnp.sin(ang)
    cos_t = jnp.tile(jnp.concatenate([cos, cos], axis=1), (1, N_HEADS))
    sin_t = jnp.tile(jnp.concatenate([-sin, sin], axis=1), (1, N_HEADS))
    return cos_t, sin_t


def _lane_pad(v, width):
    v = v.reshape(1, -1)
    return jnp.pad(v, ((0, 0), (0, width - v.shape[1])))


def _rwkv_params(mu, w0, w2, a0, a2, g2, k_k, k_a, r_k, ln_w, ln_b):
    rank_w, rank_a, rank_g = w2.shape[1], a2.shape[1], g2.shape[0]
    w2p = jnp.zeros((2, LANES_V7X, GROUP), F32)
    a2p = jnp.zeros((2, LANES_V7X, GROUP), F32)
    for d in range(2):
        w2p = w2p.at[d, d * rank_w:(d + 1) * rank_w].set(w2[d])
        a2p = a2p.at[d, 2 * rank_w + d * rank_a:2 * rank_w + (d + 1) * rank_a].set(a2[d])
    g2p = jnp.zeros((LANES_V7X, GROUP), F32).at[0:rank_g].set(g2)
    return dict(mu=_lane_pad(mu, 4 * GROUP), w0=w0.reshape(2, 1, GROUP), w2=w2p.astype(BF16),
                a0=a0.reshape(2, 1, GROUP), a2=a2p.astype(BF16), g2=g2p.astype(BF16),
                k_k=k_k.reshape(1, GROUP), k_a=k_a.reshape(1, GROUP), r_k=r_k.reshape(1, GROUP),
                ln_w=ln_w.reshape(1, GROUP), ln_b=ln_b.reshape(1, GROUP))


def kernel(x, norm_g, ffn_w_gate, ffn_w_up, ffn_w_down, w_in, w_out, conv_w, conv_b, ret_decay_logit, rwkv_mu, rwkv_w0, rwkv_w2, rwkv_a0, rwkv_a2, rwkv_g2, rwkv_k_k, rwkv_k_a, rwkv_r_k, rwkv_ln_w, rwkv_ln_b, mlstm_i_bias, mlstm_f_bias, mlstm_norm_w):
    batch, seq, _ = x.shape
    depth = norm_g.shape[0]
    n_tok = batch * seq
    assert seq % ROW_TILE == 0 and seq % (RET_CHUNK * RET_SUBCHUNKS) == 0
    assert seq % (RWKV_CHUNK * RWKV_SUBCHUNKS) == 0
    cos_t, sin_t = _rope_tables(seq)
    bdh = _block_diag_ones()
    expand = _gate_expand()
    xf = x.reshape(n_tok, D_MODEL)
    ffn_w = (ffn_w_gate.astype(BF16), ffn_w_up.astype(BF16), ffn_w_down.astype(BF16))
    w_out_bf = w_out.astype(BF16)
    for l in range(depth):
        g = norm_g[l].reshape(6, 1, D_MODEL)
        xf = _ffn(xf, g[0], g[1], ffn_w, l, 0)
        z = _inproj(xf, g[2], _relayout_w_in(w_in[l]))
        z3 = z.reshape(batch, seq, Z_COLS)
        log_gamma = jax.nn.log_sigmoid(ret_decay_logit[l].astype(F32))
        y_ret, y_mlstm = _ret_mlstm(
            z3, cos_t, sin_t, log_gamma, _lane_pad(mlstm_i_bias[l], LANES_V7X),
            _lane_pad(mlstm_f_bias[l], LANES_V7X), mlstm_norm_w[l].reshape(1, GROUP), expand, bdh)
        y_rwkv = _rwkv(z3, _rwkv_params(
            rwkv_mu[l], rwkv_w0[l], rwkv_w2[l], rwkv_a0[l], rwkv_a2[l], rwkv_g2[l], rwkv_k_k[l],
            rwkv_k_a[l], rwkv_r_k[l], rwkv_ln_w[l], rwkv_ln_b[l]), bdh)
        conv_w8 = jnp.pad(conv_w[l], ((0, SUBLANES_V7X - conv_w.shape[1]), (0, 0)))
        xf = _outproj(xf, z, y_ret.reshape(n_tok, GROUP), y_rwkv.reshape(n_tok, GROUP),
                      y_mlstm.reshape(n_tok, GROUP), conv_w8, conv_b[l].reshape(1, GROUP),
                      w_out_bf, l, g[3], seq)
        xf = _ffn(xf, g[4], g[5], ffn_w, l, 1)
    return xf.reshape(batch, seq, D_MODEL)
```

```python
import functools

import numpy as np
import jax
import jax.numpy as jnp
from jax import lax
from jax.experimental import pallas as pl
from jax.experimental.pallas import tpu as pltpu

F32 = jnp.float32
BF16 = jnp.bfloat16

D_MODEL = 1024
GROUP = 256
N_HEADS = 4
HEAD_DIM = 64
D_FF = 2816
RWKV_COLS = 960
RMS_EPS = 1e-6
RWKV_GN_EPS = 64e-5

LANES_V7X = 128
SUBLANES_V7X = 8
VMEM_LIMIT_BYTES = 56 * 1024 * 1024

ROW_TILE = 512
OUTPROJ_ROW_TILE = 1024
FFN_CHUNK = 256
RET_CHUNK = 128
RET_SUBCHUNKS = 2
RWKV_CHUNK = 64
RWKV_SUBCHUNKS = 4
RWKV_PREP_SPACING = 3

HALF = LANES_V7X
N_HALVES = GROUP // HALF
HEADS_PER_HALF = HALF // HEAD_DIM

Z_COLS = 4096
ZB_Q, ZB_K, ZB_V, ZB_G = 0, 1, 2, 3
ZC_BLOCK_1024 = 1
ZA_B, ZA_C, ZA_H = 8, 9, 10
ZD_Q, ZD_K, ZD_V, ZD_O = 11, 12, 13, 14
ZD_GI_128, ZD_GF_128 = 30, 31


def _mm(a, b):
    return jnp.dot(a.astype(BF16), b.astype(BF16), preferred_element_type=F32)


def _mm_nt(a, b):
    return lax.dot_general(a.astype(BF16), b.astype(BF16), (((1,), (1,)), ((), ())),
                           preferred_element_type=F32)


def _mm_tn(a, b):
    return lax.dot_general(a.astype(BF16), b.astype(BF16), (((0,), (0,)), ((), ())),
                           preferred_element_type=F32)


def _bf16_terms(x, terms):
    out = []
    for _ in range(terms - 1):
        part = x.astype(BF16)
        out.append(part)
        x = x - part.astype(F32)
    out.append(x.astype(BF16))
    return out


def _sel_mm(sel, x, terms=3):
    s = sel.astype(BF16)
    return sum(jnp.dot(s, part, preferred_element_type=F32) for part in _bf16_terms(x, terms))


def _mm_sel(x, sel, terms=3):
    s = sel.astype(BF16)
    return sum(jnp.dot(part, s, preferred_element_type=F32) for part in _bf16_terms(x, terms))


def _head_sum(x, bdh):
    return jnp.concatenate([_mm(x[:, _lanes_of(h)], bdh) for h in range(N_HALVES)], axis=1)


def _rows_of(stacked, b, n):
    return stacked[b * n:(b + 1) * n]


def _div_pow2(x, d):
    shift = d.bit_length() - 1
    assert 1 << shift == d
    return lax.shift_right_logical(x, shift)


def _rms_norm(x, g):
    return x * lax.rsqrt(jnp.mean(x * x, axis=-1, keepdims=True) + RMS_EPS) * g


def _softplus(x):
    return jnp.maximum(x, 0.0) + jnp.log(1.0 + jnp.exp(-jnp.abs(x)))


def _sigmoid(x):
    return 1.0 / (1.0 + jnp.exp(-x))


def _order_masks(n, reverse, copies=1):
    row = lax.broadcasted_iota(jnp.int32, (n, copies * n), 0)
    col = lax.broadcasted_iota(jnp.int32, (n, copies * n), 1)
    if copies > 1:
        assert n & (n - 1) == 0
        col = col & (n - 1)
    if reverse:
        return col >= row, col > row
    return col <= row, col < row


def _sub_head_masks(dtype):
    lane = lax.broadcasted_iota(jnp.int32, (1, HALF), 1)
    return [((lane >= j * HEAD_DIM) & (lane < (j + 1) * HEAD_DIM)).astype(dtype)
            for j in range(HEADS_PER_HALF)]


def _lanes_of(half):
    return slice(half * HALF, (half + 1) * HALF)


def _units_and_chains(nb):
    units = [(b, half) for b in range(nb) for half in range(N_HALVES)]
    chains = [(b, half, j) for (b, half) in units for j in range(HEADS_PER_HALF)]
    return units, chains


def _join_halves(per_unit, b):
    return jnp.concatenate([per_unit[(b, half)] for half in range(N_HALVES)], axis=1)


def _params(*sem):
    return pltpu.CompilerParams(dimension_semantics=sem, vmem_limit_bytes=VMEM_LIMIT_BYTES)


def _ffn_body(x_ref, gin_ref, gout_ref, wg_ref, wu_ref, wd_ref, o_ref, acc_ref):
    x = x_ref[...]
    h = _rms_norm(x, gin_ref[...]).astype(BF16)
    for c in range(D_FF // FFN_CHUNK):
        lo = c * FFN_CHUNK
        gate = jnp.dot(h, wg_ref[:, lo:lo + FFN_CHUNK], preferred_element_type=F32)
        up = jnp.dot(h, wu_ref[:, lo:lo + FFN_CHUNK], preferred_element_type=F32)
        act = (gate * _sigmoid(gate) * up).astype(BF16)
        part = jnp.dot(act, wd_ref[lo:lo + FFN_CHUNK, :], preferred_element_type=F32)
        if c == 0:
            acc_ref[...] = part
        else:
            acc_ref[...] += part
    o_ref[...] = x + 0.5 * _rms_norm(acc_ref[...], gout_ref[...])


def _ffn(x, g_in, g_out, weights, layer, which):
    w_gate, w_up, w_down = weights
    n = x.shape[0]
    tm = min(ROW_TILE, n)
    const = lambda i: (0, 0)
    pick = lambda i: (layer, which, 0, 0)
    return pl.pallas_call(
        _ffn_body,
        grid=(n // tm,),
        in_specs=[
            pl.BlockSpec((tm, D_MODEL), lambda i: (i, 0)),
            pl.BlockSpec((1, D_MODEL), const),
            pl.BlockSpec((1, D_MODEL), const),
            pl.BlockSpec((None, None, D_MODEL, D_FF), pick),
            pl.BlockSpec((None, None, D_MODEL, D_FF), pick),
            pl.BlockSpec((None, None, D_FF, D_MODEL), pick),
        ],
        out_specs=pl.BlockSpec((tm, D_MODEL), lambda i: (i, 0)),
        out_shape=jax.ShapeDtypeStruct((n, D_MODEL), F32),
        scratch_shapes=[pltpu.VMEM((tm, D_MODEL), F32)],
        compiler_params=_params("parallel"),
        name="ffn",
    )(x, g_in, g_out, w_gate, w_up, w_down)


def _inproj_body(x_ref, g_ref, w_ref, z_ref):
    h = _rms_norm(x_ref[...], g_ref[...]).astype(BF16)
    z_ref[...] = jnp.dot(h, w_ref[...], preferred_element_type=F32)


def _inproj(x, g, w):
    n = x.shape[0]
    tm = min(ROW_TILE, n)
    return pl.pallas_call(
        _inproj_body,
        grid=(n // tm,),
        in_specs=[
            pl.BlockSpec((tm, D_MODEL), lambda i: (i, 0)),
            pl.BlockSpec((1, D_MODEL), lambda i: (0, 0)),
            pl.BlockSpec((D_MODEL, Z_COLS), lambda i: (0, 0)),
        ],
        out_specs=pl.BlockSpec((tm, Z_COLS), lambda i: (i, 0)),
        out_shape=jax.ShapeDtypeStruct((n, Z_COLS), F32),
        compiler_params=_params("parallel"),
        name="inproj",
    )(x, g, w)


def _chunk_index(nc, reverse):
    if reverse:
        return lambda c: nc - 1 - c
    return lambda c: c


def _seq_block(batch, rows, width, nc, reverse, col):
    cidx = _chunk_index(nc, reverse)
    return pl.BlockSpec((batch, rows, width), lambda c: (0, cidx(c), col))


def _const_block(shape):
    return pl.BlockSpec(shape, lambda c: (0,) * len(shape))


def _ret_section(in_refs, o_ref, st_ref, rows, *, reverse, nb):
    if reverse:
        lgs_ref, q_ref, k_ref, v_ref, cos_ref, sin_ref, lgv_ref, bdh_ref, g_ref, fwd_ref = in_refs
    else:
        lgs_ref, q_ref, k_ref, v_ref, cos_ref, sin_ref, lgv_ref, bdh_ref = in_refs
    n = RET_CHUNK
    lane = lax.broadcasted_iota(jnp.int32, (1, GROUP), 1)
    first_half = (lane % HEAD_DIM) < (HEAD_DIM // 2)
    cos = cos_ref[rows, :]
    sin = sin_ref[rows, :]

    def rot(t):
        swapped = jnp.where(first_half, pltpu.roll(t, GROUP - HEAD_DIM // 2, 1),
                            pltpu.roll(t, HEAD_DIM // 2, 1))
        return t * cos + swapped * sin

    lg = lgv_ref[...]
    ti = lax.broadcasted_iota(jnp.int32, (n, 1), 0).astype(F32)
    if reverse:
        q_exp, k_exp = n - ti, ti
    else:
        q_exp, k_exp = ti + 1.0, n - 1.0 - ti
    q_dec = jnp.exp(q_exp * lg)
    k_dec = jnp.exp(k_exp * lg)
    c_dec = jnp.exp(float(n) * lg)
    row = lax.broadcasted_iota(jnp.int32, (n, n), 0)
    col = lax.broadcasted_iota(jnp.int32, (n, n), 1)
    if reverse:
        rel, msk = (col - row).astype(F32), col > row
    else:
        rel, msk = (row - col).astype(F32), col <= row
    intra = [jnp.where(msk, jnp.exp(jnp.where(msk, rel, 0.0) * lgs_ref[h]), 0.0) for h in range(N_HEADS)]
    sub_masks = _sub_head_masks(BF16)
    units, chains = _units_and_chains(nb)

    q = [(rot(q_ref[b, rows, :]) * HEAD_DIM ** -0.5).astype(BF16) for b in range(nb)]
    k_rot = [rot(k_ref[b, rows, :]) for b in range(nb)]
    k = [t.astype(BF16) for t in k_rot]
    k_scaled = [(t * k_dec).astype(BF16) for t in k_rot]
    v = [v_ref[b, rows, :].astype(BF16) for b in range(nb)]
    yield
    state = {u: st_ref[u[0], u[1]] for u in units}
    inter = {(b, h): _mm(q[b][:, _lanes_of(h)], state[(b, h)]) for (b, h) in units}
    yield
    s = {(b, h, j): _mm_nt(q[b][:, _lanes_of(h)] * sub_masks[j], k[b][:, _lanes_of(h)])
         for (b, h, j) in chains}
    yield
    p = {(b, h, j): (s[(b, h, j)] * intra[h * HEADS_PER_HALF + j]).astype(BF16) for (b, h, j) in chains}
    intra_out = {
        (b, h): _mm(jnp.concatenate([p[(b, h, j)] for j in range(HEADS_PER_HALF)], axis=1),
                    jnp.concatenate([v[b][:, _lanes_of(h)] * sub_masks[j] for j in range(HEADS_PER_HALF)],
                                    axis=0))
        for (b, h) in units}
    yield
    for (b, h) in units:
        sl = _lanes_of(h)
        st_ref[b, h] = c_dec[:, sl] * state[(b, h)] + bdh_ref[...] * _mm_tn(k_scaled[b][:, sl], v[b][:, sl])
    yield
    outs = [q_dec * _join_halves(inter, b) + _join_halves(intra_out, b) for b in range(nb)]
    if reverse:
        tot = jnp.concatenate([fwd_ref[b, rows, :] + outs[b] for b in range(nb)], axis=0)
        normed = tot * lax.rsqrt(_head_sum(tot * tot, bdh_ref[...]) * (1.0 / HEAD_DIM) + RMS_EPS)
        for b in range(nb):
            g = g_ref[b, rows, :]
            o_ref[b, rows, :] = (g * _sigmoid(g) * _rows_of(normed, b, n)).astype(o_ref.dtype)
    else:
        for b in range(nb):
            o_ref[b, rows, :] = outs[b]


def _ret_operands(z3, cos_t, sin_t, log_gamma, bdh, rev, fwd):
    batch, seq, _ = z3.shape
    n = RET_CHUNK * RET_SUBCHUNKS
    nc = seq // n
    blk = lambda col: _seq_block(batch, n, GROUP, nc, rev, col)
    cidx = _chunk_index(nc, rev)
    tab = pl.BlockSpec((n, GROUP), lambda c: (cidx(c), 0))
    d = int(rev)
    lgv = jnp.repeat(log_gamma, HEAD_DIM, axis=1)
    specs = [pl.BlockSpec(memory_space=pltpu.SMEM), blk(ZB_Q), blk(ZB_K), blk(ZB_V), tab, tab,
             _const_block((1, GROUP)), _const_block((HALF, HALF))]
    operands = [log_gamma[d], z3, z3, z3, cos_t, sin_t, lgv[d:d + 1], bdh]
    if rev:
        specs += [blk(ZB_G), blk(0)]
        operands += [z3, fwd]
    return specs, operands


def _mlstm_section(in_refs, o_ref, c_ref, n_ref, m_ref, rows, *, reverse, nb):
    if reverse:
        (q_ref, k_ref, v_ref, gi_ref, gf_ref, ib_ref, fb_ref, ed_ref, bdh_ref, o_gate_ref, fwd_ref,
         nw_ref) = in_refs
    else:
        q_ref, k_ref, v_ref, gi_ref, gf_ref, ib_ref, fb_ref, ed_ref, bdh_ref = in_refs
    direction = int(reverse)
    n = RET_CHUNK
    incl, _ = _order_masks(n, reverse)
    sub_bf = _sub_head_masks(BF16)
    sub_f32 = _sub_head_masks(F32)
    units, chains = _units_and_chains(nb)
    ed = ed_ref[...]
    last = 0 if reverse else n - 1

    qf = [q_ref[b, rows, :] * HEAD_DIM ** -0.5 for b in range(nb)]
    q = [t.astype(BF16) for t in qf]
    kf = [k_ref[b, rows, :] for b in range(nb)]
    k = [t.astype(BF16) for t in kf]
    v = [v_ref[b, rows, :].astype(BF16) for b in range(nb)]
    gi = [gi_ref[b, rows, :] + ib_ref[...] for b in range(nb)]
    lf = [-_softplus(-(gf_ref[b, rows, :] + fb_ref[...])) for b in range(nb)]
    yield
    bcum_lanes = _sel_mm(incl, jnp.concatenate(lf, axis=1))
    bcum = [bcum_lanes[:, b * LANES_V7X:(b + 1) * LANES_V7X] for b in range(nb)]
    yield
    row_terms = [(gi[b] - bcum[b]).T for b in range(nb)]
    li_all = _mm_sel(jnp.concatenate(gi, axis=0), ed)
    yield
    bcum_all = _mm_sel(jnp.concatenate(bcum, axis=0), ed)
    li_x = [_rows_of(li_all, b, n) for b in range(nb)]
    bcum_x = [_rows_of(bcum_all, b, n) for b in range(nb)]
    m_x = [m_ref[b] for b in range(nb)]
    n_x = [n_ref[b] for b in range(nb)]
    yield
    head_of = lambda c: c[1] * HEADS_PER_HALF + c[2]
    gate_lane = lambda c: direction * N_HEADS + head_of(c)
    colv = {c: bcum[c[0]][:, gate_lane(c):gate_lane(c) + 1] for c in chains}
    col_b = {c: jnp.broadcast_to(colv[c], (n, n)) for c in chains}
    yield
    d_log = {c: jnp.where(incl, col_b[c] + row_terms[c[0]][gate_lane(c):gate_lane(c) + 1, :], -jnp.inf)
             for c in chains}
    row_max = {c: jnp.max(d_log[c], axis=1, keepdims=True) for c in chains}
    yield
    mt = {c: jnp.maximum(colv[c] + m_x[c[0]][:, head_of(c) * HEAD_DIM:head_of(c) * HEAD_DIM + 1], row_max[c])
          for c in chains}
    mt_b = {c: jnp.broadcast_to(mt[c], (n, n)) for c in chains}
    yield
    dm = {c: jnp.exp(d_log[c] - mt_b[c]) for c in chains}
    cmat = {u: c_ref[u[0], u[1]] for u in units}
    s = {(b, h, j): (_mm_nt(q[b][:, _lanes_of(h)] * sub_bf[j], k[b][:, _lanes_of(h)]) * dm[(b, h, j)]
                     ).astype(BF16) for (b, h, j) in chains}
    yield
    ones_rows = _div_pow2(lax.broadcasted_iota(jnp.int32, (HEADS_PER_HALF * n, HALF), 0), n)
    ones_lanes = _div_pow2(lax.broadcasted_iota(jnp.int32, (HEADS_PER_HALF * n, HALF), 1), HEAD_DIM)
    head_ones = (ones_rows == ones_lanes).astype(BF16)
    num_u, den_u = {}, {}
    for (b, h) in units:
        v_h = v[b][:, _lanes_of(h)]
        rhs = jnp.concatenate([jnp.concatenate([v_h * sub_bf[j] for j in range(HEADS_PER_HALF)], axis=0),
                               head_ones], axis=1)
        nd = _mm(jnp.concatenate([s[(b, h, j)] for j in range(HEADS_PER_HALF)], axis=1), rhs)
        num_u[(b, h)] = nd[:, 0:HALF]
        den_u[(b, h)] = nd[:, HALF:2 * HALF]
    yield
    inter_q = {(b, h): _mm(q[b][:, _lanes_of(h)], cmat[(b, h)]) for (b, h) in units}
    q_dot_n = _head_sum(jnp.concatenate([qf[b] * n_x[b] for b in range(nb)], axis=0), bdh_ref[...])
    mt_u = {(b, h): mt[(b, h, 0)] * sub_f32[0] + mt[(b, h, 1)] * sub_f32[1] for (b, h) in units}
    yield

    outs = []
    for b in range(nb):
        mt_x = _join_halves(mt_u, b)
        inter_w = jnp.exp(bcum_x[b] + m_x[b] - mt_x)
        num = _join_halves(num_u, b) + inter_w * _join_halves(inter_q, b)
        den = _join_halves(den_u, b) + inter_w * _rows_of(q_dot_n, b, n)
        outs.append(num / jnp.maximum(jnp.abs(den), jnp.exp(-mt_x)))
        b_end = bcum_x[b][last:last + 1, :]
        w_log = b_end - bcum_x[b] + li_x[b]
        m_new = jnp.maximum(b_end + m_x[b], jnp.max(w_log, axis=0, keepdims=True))
        old_w = jnp.exp(b_end + m_x[b] - m_new)
        new_w = jnp.exp(w_log - m_new)
        k_w = (kf[b] * new_w).astype(BF16)
        for h in range(N_HALVES):
            sl = _lanes_of(h)
            c_ref[b, h] = old_w[:, sl] * cmat[(b, h)] + bdh_ref[...] * _mm_tn(k_w[:, sl], v[b][:, sl])
        n_ref[b] = old_w * n_x[b] + jnp.sum(new_w * kf[b], axis=0, keepdims=True)
        m_ref[b] = m_new
    if reverse:
        tot = jnp.concatenate([fwd_ref[b, rows, :] + outs[b] for b in range(nb)], axis=0)
        normed = tot * lax.rsqrt(_head_sum(tot * tot, bdh_ref[...]) * (1.0 / HEAD_DIM) + RMS_EPS)
        for b in range(nb):
            o_ref[b, rows, :] = (_sigmoid(o_gate_ref[b, rows, :]) * _rows_of(normed, b, n)
                                 * nw_ref[...]).astype(o_ref.dtype)
    else:
        for b in range(nb):
            o_ref[b, rows, :] = outs[b]


def _mlstm_operands(z3, i_bias, f_bias, norm_w, expand, bdh, rev, fwd):
    batch, seq, _ = z3.shape
    n = RET_CHUNK * RET_SUBCHUNKS
    nc = seq // n
    blk = lambda col: _seq_block(batch, n, GROUP, nc, rev, col)
    gblk = lambda col: _seq_block(batch, n, LANES_V7X, nc, rev, col)
    specs = [blk(ZD_Q), blk(ZD_K), blk(ZD_V), gblk(ZD_GI_128), gblk(ZD_GF_128),
             _const_block((1, LANES_V7X)), _const_block((1, LANES_V7X)),
             _const_block((LANES_V7X, GROUP)), _const_block((HALF, HALF))]
    operands = [z3, z3, z3, z3, z3, i_bias, f_bias, expand[int(rev)], bdh]
    if rev:
        specs += [blk(ZD_O), blk(0), _const_block((1, GROUP))]
        operands += [z3, fwd, norm_w]
    return specs, operands


def _in_sequence(sections):
    for section in sections:
        yield from section


def _ret_mlstm_body(*refs, reverse, nb, n_ret_in):
    n_in = len(refs) - 6
    ret_o, mlstm_o, st_ref, c_ref, n_ref, m_ref = refs[n_in:]

    @pl.when(pl.program_id(0) == 0)
    def _():
        for ref in (st_ref, c_ref, n_ref, m_ref):
            ref[...] = jnp.zeros_like(ref)

    order = list(range(RET_SUBCHUNKS))
    if reverse:
        order.reverse()
    rows = lambda sub: slice(sub * RET_CHUNK, (sub + 1) * RET_CHUNK)
    _trace_interleaved(
        _in_sequence([_mlstm_section(refs[n_ret_in:n_in], mlstm_o, c_ref, n_ref, m_ref, rows(sub),
                                     reverse=reverse, nb=nb) for sub in order]),
        _in_sequence([_ret_section(refs[:n_ret_in], ret_o, st_ref, rows(sub), reverse=reverse, nb=nb)
                      for sub in order]))


def _ret_mlstm(z3, cos_t, sin_t, log_gamma, i_bias, f_bias, norm_w, expand, bdh):
    batch, seq, _ = z3.shape
    n = RET_CHUNK * RET_SUBCHUNKS
    nc = seq // n
    scratch = [pltpu.VMEM((batch, N_HALVES, HALF, HALF), F32), pltpu.VMEM((batch, N_HALVES, HALF, HALF), F32),
               pltpu.VMEM((batch, 1, GROUP), F32), pltpu.VMEM((batch, 1, GROUP), F32)]
    partial_out = (None, None)
    for rev in (False, True):
        ret_specs, ret_ops = _ret_operands(z3, cos_t, sin_t, log_gamma, bdh, rev, partial_out[0])
        ml_specs, ml_ops = _mlstm_operands(z3, i_bias, f_bias, norm_w, expand, bdh, rev, partial_out[1])
        out_blk = _seq_block(batch, n, GROUP, nc, rev, 0)
        out_shape = jax.ShapeDtypeStruct((batch, seq, GROUP), BF16 if rev else F32)
        partial_out = pl.pallas_call(
            functools.partial(_ret_mlstm_body, reverse=rev, nb=batch, n_ret_in=len(ret_specs)),
            grid=(nc,), in_specs=ret_specs + ml_specs, out_specs=[out_blk, out_blk],
            out_shape=[out_shape, out_shape], scratch_shapes=scratch,
            compiler_params=_params("arbitrary"), name="ret_mlstm_bwd" if rev else "ret_mlstm_fwd",
        )(*ret_ops, *ml_ops)
    return partial_out


def _trace_all(gen):
    try:
        while True:
            next(gen)
    except StopIteration as stop:
        return stop.value


def _pause(stages):
    for _ in range(stages):
        yield


def _interleaved(main, side):
    live = [main] if side is None else [main, side]
    side_value = None
    while live:
        for gen in list(live):
            try:
                next(gen)
            except StopIteration as stop:
                live.remove(gen)
                if gen is side:
                    side_value = stop.value
            yield
    return side_value


def _trace_interleaved(main, side):
    return _trace_all(_interleaved(main, side))


def _rwkv_body(*refs, reverse, n_blocks, nb):
    if reverse:
        (z_ref, zp_ref, zn_ref, mu_ref, w0_ref, w2_ref, a0_ref, a2_ref, kk_ref, ka_ref, rk_ref,
         bdh_ref, g2_ref, lnw_ref, lnb_ref, wkvf_ref, bonf_ref, o_ref, s_ref) = refs
    else:
        (z_ref, zp_ref, zn_ref, mu_ref, w0_ref, w2_ref, a0_ref, a2_ref, kk_ref, ka_ref, rk_ref,
         bdh_ref, wkv_ref, bon_ref, s_ref) = refs
    step = pl.program_id(0)
    block = (n_blocks - 1 - step) if reverse else step

    @pl.when(step == 0)
    def _():
        s_ref[...] = jnp.zeros_like(s_ref)

    n = RWKV_CHUNK
    incl2, strict2 = _order_masks(n, reverse, copies=2)
    sub_masks = _sub_head_masks(BF16)
    units, chains = _units_and_chains(nb)

    def prepare(sub):
        return _rwkv_prepare(z_ref, zp_ref, zn_ref, sub, block, n_blocks, reverse, nb, mu_ref, w0_ref, w2_ref,
                             a0_ref, a2_ref, kk_ref, ka_ref, rk_ref, bdh_ref)

    def solve(prep, sub):
        rows = slice(sub * n, (sub + 1) * n)
        operand = lambda name, b, half: prep[name][b][:, _lanes_of(half)]
        state = {u: s_ref[u[0], u[1]] for u in units}
        xo = {u: _mm_nt(operand("ar", *u), state[u]) for u in units}
        yield
        ar_m = {c: operand("ar", c[0], c[1]) * sub_masks[c[2]] for c in chains}
        v_m = {c: operand("v", c[0], c[1]) * sub_masks[c[2]] for c in chains}
        bk = {u: jnp.concatenate([operand("b_t", *u), operand("k_t", *u)], axis=0) for u in units}
        scores = {c: _mm_nt(ar_m[c], bk[c[0:2]]) for c in chains}
        yield
        low = {c: jnp.where(strict2, scores[c][0:n], 0.0).astype(BF16) for c in chains}
        out_w = {c: jnp.where(incl2, scores[c][n:2 * n], 0.0).astype(BF16) for c in chains}
        zeros_n = jnp.zeros((n, HALF), BF16)
        lak_v = {c: _mm(low[c], jnp.concatenate([zeros_n, v_m[c]], axis=0)) for c in chains}
        x = {u: xo[u][0:n] + lak_v[u + (0,)] + lak_v[u + (1,)] for u in units}
        yield
        p = {c: low[c][:, 0:n] for c in chains}
        u_c = {c: x[c[0:2]] * sub_masks[c[2]].astype(F32) for c in chains}
        u_c = {c: u_c[c] + _mm(p[c], u_c[c]) for c in chains}
        yield
        span = 1
        while 2 * span < n:
            p = {c: _mm(p[c], p[c]).astype(BF16) for c in chains}
            yield
            u_c = {c: u_c[c] + _mm(p[c], u_c[c]) for c in chains}
            yield
            span *= 2
        o_c = {c: _mm(out_w[c], jnp.concatenate([u_c[c].astype(BF16), v_m[c]], axis=0)) for c in chains}
        yield
        for u in units:
            b, half = u
            u_all = u_c[u + (0,)] + u_c[u + (1,)]
            upd = (_mm_tn(u_all, operand("b_t", b, half))
                   + _mm_tn(operand("v", b, half), operand("k_t", b, half)))
            s_ref[b, half] = prep["gam_end"][b][:, _lanes_of(half)] * (state[u] + bdh_ref[...] * upd)
        yield
        wkv_u = {u: xo[u][n:2 * n] + o_c[u + (0,)] + o_c[u + (1,)] for u in units}
        if reverse:
            bdh = bdh_ref[...]
            wkv = jnp.concatenate([wkvf_ref[b, rows, :] + _join_halves(wkv_u, b) for b in range(nb)], axis=0)
            cen = wkv - _head_sum(wkv, bdh) * (1.0 / HEAD_DIM)
            var = _head_sum(cen * cen, bdh) * (1.0 / HEAD_DIM)
            gate = _mm(_sigmoid(prep["lat2"]), g2_ref[...])
            out = (cen * lax.rsqrt(var + RWKV_GN_EPS) * lnw_ref[...] + lnb_ref[...] + prep["bonus"]) * gate
            for b in range(nb):
                o_ref[b, rows, :] = (_rows_of(out, b, n)
                                     + bonf_ref[b, rows, :] * _rows_of(gate, b, n)).astype(o_ref.dtype)
        else:
            for b in range(nb):
                wkv_ref[b, rows, :] = _join_halves(wkv_u, b)
                bon_ref[b, rows, :] = _rows_of(prep["bonus"], b, n)

    order = list(range(RWKV_SUBCHUNKS))
    if reverse:
        order.reverse()
    prep = _trace_all(prepare(order[0]))
    for i, sub in enumerate(order):
        following = prepare(order[i + 1]) if i + 1 < len(order) else None
        prep = _trace_interleaved(solve(prep, sub), following)


def _rwkv_prepare(z_ref, zp_ref, zn_ref, sub, block, n_blocks, reverse, nb, mu_ref, w0_ref, w2_ref, a0_ref,
                  a2_ref, kk_ref, ka_ref, rk_ref, bdh_ref):
    n = RWKV_CHUNK
    lo, hi = sub * n, (sub + 1) * n
    ridx = lax.broadcasted_iota(jnp.int32, (n, 1), 0)
    shifted = []
    for b in range(nb):
        z = z_ref[b, lo:hi, :]
        if sub == 0:
            z_prev = jnp.where(block > 0, zp_ref[b, SUBLANES_V7X - 1:SUBLANES_V7X, :], 0.0)
        else:
            z_prev = z_ref[b, lo - 1:lo, :]
        if sub == RWKV_SUBCHUNKS - 1:
            z_next = jnp.where(block < n_blocks - 1, zn_ref[b, 0:1, :], 0.0)
        else:
            z_next = z_ref[b, hi:hi + 1, :]
        z_m1 = jnp.where(ridx == 0, z_prev, pltpu.roll(z, 1, 0))
        z_p1 = jnp.where(ridx == n - 1, z_next, pltpu.roll(z, n - 1, 0))
        shifted.append(z + mu_ref[...] * (0.5 * (z_m1 + z_p1) - z))
    zs = jnp.concatenate(shifted, axis=0)
    bdh = bdh_ref[...]
    r = zs[:, 0:GROUP]
    k = zs[:, GROUP:2 * GROUP]
    v = zs[:, 2 * GROUP:3 * GROUP]
    lat1 = zs[:, 3 * GROUP:3 * GROUP + LANES_V7X]
    lat2 = zs[:, 3 * GROUP + LANES_V7X:4 * GROUP]
    yield from _pause(RWKV_PREP_SPACING)
    w_pre = w0_ref[...] + _mm(jnp.tanh(lat1), w2_ref[...])
    log_decay = -jnp.exp(-_softplus(-w_pre) - 0.5)
    a = _sigmoid(a0_ref[...] + _mm(lat1, a2_ref[...]))
    kk_raw = k * kk_ref[...]
    yield from _pause(RWKV_PREP_SPACING)
    kk = kk_raw / jnp.maximum(jnp.sqrt(_head_sum(kk_raw * kk_raw, bdh)), 1e-12)
    kd = k * (1.0 + (a - 1.0) * ka_ref[...])
    yield from _pause(RWKV_PREP_SPACING)
    bonus = _head_sum(r * kd * rk_ref[...], bdh) * v
    yield from _pause(RWKV_PREP_SPACING)
    rows = nb * n
    row = lax.broadcasted_iota(jnp.int32, (rows, rows), 0)
    col = lax.broadcasted_iota(jnp.int32, (rows, rows), 1)
    same_seq = _div_pow2(row, n) == _div_pow2(col, n)
    before = (col >= row) if reverse else (col <= row)
    cum_incl = _sel_mm(same_seq & before, log_decay, terms=2)
    cum_excl = cum_incl - log_decay
    last = 0 if reverse else n - 1
    a_t = (-kk * jnp.exp(cum_excl)).astype(BF16)
    r_t = (r * jnp.exp(cum_incl)).astype(BF16)
    inv_g = jnp.exp(-cum_incl)
    b_t = (kk * a * inv_g).astype(BF16)
    k_t = (kd * inv_g).astype(BF16)
    vb = v.astype(BF16)
    per_seq = lambda t: [_rows_of(t, b, n) for b in range(nb)]
    return dict(ar=[jnp.concatenate([_rows_of(a_t, b, n), _rows_of(r_t, b, n)], axis=0) for b in range(nb)],
                b_t=per_seq(b_t), k_t=per_seq(k_t), v=per_seq(vb),
                gam_end=[jnp.exp(cum_incl[b * n + last:b * n + last + 1, :]) for b in range(nb)],
                bonus=bonus, lat2=lat2)


def _rwkv(z3, p, bdh):
    batch, seq, _ = z3.shape
    rows = RWKV_CHUNK * RWKV_SUBCHUNKS
    n_blocks = seq // rows
    rows8 = rows // SUBLANES_V7X
    last8 = seq // SUBLANES_V7X - 1
    wide = 4 * GROUP

    def halo(rev, offset):
        bidx = _chunk_index(n_blocks, rev)
        return pl.BlockSpec((batch, SUBLANES_V7X, wide), lambda c: (
            0, jnp.clip((bidx(c) + offset) * rows8 - (1 - offset), 0, last8), ZC_BLOCK_1024))

    vec = _const_block((1, GROUP))
    lora = _const_block((LANES_V7X, GROUP))
    out_blk = lambda rev: _seq_block(batch, rows, GROUP, n_blocks, rev, 0)
    out_shape = jax.ShapeDtypeStruct((batch, seq, GROUP), F32)

    def specs(rev):
        return [_seq_block(batch, rows, wide, n_blocks, rev, ZC_BLOCK_1024), halo(rev, 0), halo(rev, 1),
                _const_block((1, wide)), vec, lora, vec, lora, vec, vec, vec, _const_block((HALF, HALF))]

    common = dict(grid=(n_blocks,), scratch_shapes=[pltpu.VMEM((batch, N_HALVES, HALF, HALF), F32)],
                  compiler_params=_params("arbitrary"))
    wkv_f, bon_f = pl.pallas_call(
        functools.partial(_rwkv_body, reverse=False, n_blocks=n_blocks, nb=batch),
        in_specs=specs(False), out_specs=[out_blk(False), out_blk(False)],
        out_shape=[out_shape, out_shape], name="rwkv_fwd", **common,
    )(z3, z3, z3, p["mu"], p["w0"][0], p["w2"][0], p["a0"][0], p["a2"][0], p["k_k"], p["k_a"], p["r_k"], bdh)
    return pl.pallas_call(
        functools.partial(_rwkv_body, reverse=True, n_blocks=n_blocks, nb=batch),
        in_specs=specs(True) + [lora, vec, vec, out_blk(True), out_blk(True)],
        out_specs=out_blk(True), out_shape=jax.ShapeDtypeStruct((batch, seq, GROUP), BF16),
        name="rwkv_bwd", **common,
    )(z3, z3, z3, p["mu"], p["w0"][1], p["w2"][1], p["a0"][1], p["a2"][1], p["k_k"], p["k_a"], p["r_k"], bdh,
      p["g2"], p["ln_w"], p["ln_b"], wkv_f, bon_f)


def _outproj_body(x_ref, cb_ref, cc_ref, ch_ref, ccp_ref, chp_ref, ccn_ref, chn_ref, yb_ref, yc_ref, yd_ref,
                  cw_ref, cbias_ref, w_ref, g_ref, o_ref, *, tiles_per_seq):
    tm = x_ref.shape[0]
    t = pl.program_id(0) % tiles_per_seq
    u = cc_ref[...] * ch_ref[...]
    last8 = SUBLANES_V7X - 1
    u_prev = jnp.where(t > 0, ccp_ref[last8:last8 + 1, :] * chp_ref[last8:last8 + 1, :], 0.0)
    u_next = jnp.where(t < tiles_per_seq - 1, ccn_ref[0:1, :] * chn_ref[0:1, :], 0.0)
    ridx = lax.broadcasted_iota(jnp.int32, (tm, 1), 0)
    u_m1 = jnp.where(ridx == 0, u_prev, pltpu.roll(u, 1, 0))
    u_p1 = jnp.where(ridx == tm - 1, u_next, pltpu.roll(u, tm - 1, 0))
    cw = cw_ref[...]
    conv = cw[0:1, :] * u_m1 + cw[1:2, :] * u + cw[2:3, :] * u_p1 + cbias_ref[...]
    y_conv = cb_ref[...] * conv
    y = (_mm(y_conv, w_ref[0:GROUP, :]) + _mm(yb_ref[...], w_ref[GROUP:2 * GROUP, :])
         + _mm(yc_ref[...], w_ref[2 * GROUP:3 * GROUP, :]) + _mm(yd_ref[...], w_ref[3 * GROUP:4 * GROUP, :]))
    o_ref[...] = x_ref[...] + _rms_norm(y, g_ref[...])


def _outproj(x, z, y_ret, y_rwkv, y_mlstm, conv_w8, conv_b, w_out, layer, g, seq):
    n = x.shape[0]
    tm = min(OUTPROJ_ROW_TILE, seq)
    tiles_per_seq = seq // tm
    rows8 = tm // SUBLANES_V7X
    last8 = n // SUBLANES_V7X - 1
    zblk = lambda col: pl.BlockSpec((tm, GROUP), lambda i: (i, col))
    prev = lambda col: pl.BlockSpec((SUBLANES_V7X, GROUP), lambda i: (jnp.maximum(i * rows8 - 1, 0), col))
    nxt = lambda col: pl.BlockSpec((SUBLANES_V7X, GROUP), lambda i: (jnp.minimum((i + 1) * rows8, last8), col))
    yblk = pl.BlockSpec((tm, GROUP), lambda i: (i, 0))
    const = lambda i: (0, 0)
    return pl.pallas_call(
        functools.partial(_outproj_body, tiles_per_seq=tiles_per_seq),
        grid=(n // tm,),
        in_specs=[pl.BlockSpec((tm, D_MODEL), lambda i: (i, 0)),
                  zblk(ZA_B), zblk(ZA_C), zblk(ZA_H), prev(ZA_C), prev(ZA_H), nxt(ZA_C), nxt(ZA_H),
                  yblk, yblk, yblk,
                  pl.BlockSpec((SUBLANES_V7X, GROUP), const), pl.BlockSpec((1, GROUP), const),
                  pl.BlockSpec((None, D_MODEL, D_MODEL), lambda i: (layer, 0, 0)),
                  pl.BlockSpec((1, D_MODEL), const)],
        out_specs=pl.BlockSpec((tm, D_MODEL), lambda i: (i, 0)),
        out_shape=jax.ShapeDtypeStruct((n, D_MODEL), F32),
        compiler_params=_params("parallel"),
        name="outproj",
    )(x, z, z, z, z, z, z, z, y_ret, y_rwkv, y_mlstm, conv_w8, conv_b, w_out, g)


def _relayout_w_in(w):
    pad = lambda c: jnp.zeros((D_MODEL, c), w.dtype)
    a, b = w[:, 0:768], w[:, 768:1792]
    c = w[:, 1792:1792 + RWKV_COLS]
    d0 = 1792 + RWKV_COLS
    d = w[:, d0:d0 + 4 * GROUP]
    gi = w[:, d0 + 4 * GROUP:d0 + 4 * GROUP + 2 * N_HEADS]
    gf = w[:, d0 + 4 * GROUP + 2 * N_HEADS:d0 + 4 * GROUP + 4 * N_HEADS]
    gpad = LANES_V7X - 2 * N_HEADS
    return jnp.concatenate([b, c, pad(4 * GROUP - RWKV_COLS), a, d, gi, pad(gpad), gf, pad(gpad)],
                           axis=1).astype(BF16)


def _block_diag_ones():
    idx = np.arange(HALF) // HEAD_DIM
    return jnp.asarray((idx[:, None] == idx[None, :]).astype(np.float32))


def _gate_expand():
    j = np.arange(LANES_V7X)[:, None]
    head = np.arange(GROUP)[None, :] // HEAD_DIM
    return jnp.asarray(np.stack([(j == d * N_HEADS + head) for d in range(2)]).astype(np.float32))


def _rope_tables(seq):
    inv = 10000.0 ** (-jnp.arange(0, HEAD_DIM, 2, dtype=F32) / HEAD_DIM)
    ang = jnp.arange(seq, dtype=F32)[:, None] * inv[None, :]
    cos, sin = jnp.cos(ang), jnp.sin(ang)
    cos_t = jnp.tile(jnp.concatenate([cos, cos], axis=1), (1, N_HEADS))
    sin_t = jnp.tile(jnp.concatenate([-sin, sin], axis=1), (1, N_HEADS))
    return cos_t, sin_t


def _lane_pad(v, width):
    v = v.reshape(1, -1)
    return jnp.pad(v, ((0, 0), (0, width - v.shape[1])))


def _rwkv_params(mu, w0, w2, a0, a2, g2, k_k, k_a, r_k, ln_w, ln_b):
    rank_w, rank_a, rank_g = w2.shape[1], a2.shape[1], g2.shape[0]
    w2p = jnp.zeros((2, LANES_V7X, GROUP), F32)
    a2p = jnp.zeros((2, LANES_V7X, GROUP), F32)
    for d in range(2):
        w2p = w2p.at[d, d * rank_w:(d + 1) * rank_w].set(w2[d])
        a2p = a2p.at[d, 2 * rank_w + d * rank_a:2 * rank_w + (d + 1) * rank_a].set(a2[d])
    g2p = jnp.zeros((LANES_V7X, GROUP), F32).at[0:rank_g].set(g2)
    return dict(mu=_lane_pad(mu, 4 * GROUP), w0=w0.reshape(2, 1, GROUP), w2=w2p.astype(BF16),
                a0=a0.reshape(2, 1, GROUP), a2=a2p.astype(BF16), g2=g2p.astype(BF16),
                k_k=k_k.reshape(1, GROUP), k_a=k_a.reshape(1, GROUP), r_k=r_k.reshape(1, GROUP),
                ln_w=ln_w.reshape(1, GROUP), ln_b=ln_b.reshape(1, GROUP))


def kernel(x, norm_g, ffn_w_gate, ffn_w_up, ffn_w_down, w_in, w_out, conv_w, conv_b, ret_decay_logit, rwkv_mu, rwkv_w0, rwkv_w2, rwkv_a0, rwkv_a2, rwkv_g2, rwkv_k_k, rwkv_k_a, rwkv_r_k, rwkv_ln_w, rwkv_ln_b, mlstm_i_bias, mlstm_f_bias, mlstm_norm_w):
    batch, seq, _ = x.shape
    depth = norm_g.shape[0]
    n_tok = batch * seq
    assert seq % OUTPROJ_ROW_TILE == 0 and seq % (RET_CHUNK * RET_SUBCHUNKS) == 0
    assert seq % (RWKV_CHUNK * RWKV_SUBCHUNKS) == 0
    cos_t, sin_t = _rope_tables(seq)
    bdh = _block_diag_ones()
    expand = _gate_expand()
    xf = x.reshape(n_tok, D_MODEL)
    ffn_w = (ffn_w_gate.astype(BF16), ffn_w_up.astype(BF16), ffn_w_down.astype(BF16))
    w_out_bf = w_out.astype(BF16)
    for l in range(depth):
        g = norm_g[l].reshape(6, 1, D_MODEL)
        xf = _ffn(xf, g[0], g[1], ffn_w, l, 0)
        z = _inproj(xf, g[2], _relayout_w_in(w_in[l]))
        z3 = z.reshape(batch, seq, Z_COLS)
        log_gamma = jax.nn.log_sigmoid(ret_decay_logit[l].astype(F32))
        y_ret, y_mlstm = _ret_mlstm(
            z3, cos_t, sin_t, log_gamma, _lane_pad(mlstm_i_bias[l], LANES_V7X),
            _lane_pad(mlstm_f_bias[l], LANES_V7X), mlstm_norm_w[l].reshape(1, GROUP), expand, bdh)
        y_rwkv = _rwkv(z3, _rwkv_params(
            rwkv_mu[l], rwkv_w0[l], rwkv_w2[l], rwkv_a0[l], rwkv_a2[l], rwkv_g2[l], rwkv_k_k[l],
            rwkv_k_a[l], rwkv_r_k[l], rwkv_ln_w[l], rwkv_ln_b[l]), bdh)
        conv_w8 = jnp.pad(conv_w[l], ((0, SUBLANES_V7X - conv_w.shape[1]), (0, 0)))
        xf = _outproj(xf, z, y_ret.reshape(n_tok, GROUP), y_rwkv.reshape(n_tok, GROUP),
                      y_mlstm.reshape(n_tok, GROUP), conv_w8, conv_b[l].reshape(1, GROUP),
                      w_out_bf, l, g[3], seq)
        xf = _ffn(xf, g[4], g[5], ffn_w, l, 1)
    return xf.reshape(batch, seq, D_MODEL)
```

```python
import functools

import numpy as np
import jax
import jax.numpy as jnp
from jax import lax
from jax.experimental import pallas as pl
from jax.experimental.pallas import tpu as pltpu

F32 = jnp.float32
BF16 = jnp.bfloat16

D_MODEL = 1024
GROUP = 256
N_HEADS = 4
HEAD_DIM = 64
D_FF = 2816
RWKV_COLS = 960
RMS_EPS = 1e-6
RWKV_GN_EPS = 64e-5

LANES_V7X = 128
SUBLANES_V7X = 8
VMEM_LIMIT_BYTES = 56 * 1024 * 1024

ROW_TILE = 512
OUTPROJ_ROW_TILE = 1024
FFN_CHUNK = 256
RET_CHUNK = 128
RET_SUBCHUNKS = 2
RET_STAGE_GAP = 1
RWKV_CHUNK = 64
RWKV_SUBCHUNKS = 4
RWKV_PREP_SPACING = 3

HALF = LANES_V7X
N_HALVES = GROUP // HALF
HEADS_PER_HALF = HALF // HEAD_DIM

Z_COLS = 3840
ZB_Q, ZB_K, ZB_V, ZB_G = 0, 1, 2, 3
ZC_BLOCK_1024 = 1
ZA_B, ZA_C, ZA_H = 8, 9, 10
ZD_Q, ZD_K, ZD_V, ZD_O = 11, 12, 13, 14
Z_GATES_128 = 15
GATE_I_LANE = 64
GATE_F_LANE = 72


def _mm(a, b):
    return jnp.dot(a.astype(BF16), b.astype(BF16), preferred_element_type=F32)


def _mm_nt(a, b):
    return lax.dot_general(a.astype(BF16), b.astype(BF16), (((1,), (1,)), ((), ())),
                           preferred_element_type=F32)


def _mm_tn(a, b):
    return lax.dot_general(a.astype(BF16), b.astype(BF16), (((0,), (0,)), ((), ())),
                           preferred_element_type=F32)


def _bf16_terms(x, terms):
    out = []
    for _ in range(terms - 1):
        part = x.astype(BF16)
        out.append(part)
        x = x - part.astype(F32)
    out.append(x.astype(BF16))
    return out


def _sel_mm(sel, x, terms=3):
    s = sel.astype(BF16)
    return sum(jnp.dot(s, part, preferred_element_type=F32) for part in _bf16_terms(x, terms))


def _mm_sel(x, sel, terms=3):
    s = sel.astype(BF16)
    return sum(jnp.dot(part, s, preferred_element_type=F32) for part in _bf16_terms(x, terms))


def _head_sum(x, bdh):
    return jnp.concatenate([_mm(x[:, _lanes_of(h)], bdh) for h in range(N_HALVES)], axis=1)


def _rows_of(stacked, b, n):
    return stacked[b * n:(b + 1) * n]


def _div_pow2(x, d):
    shift = d.bit_length() - 1
    assert 1 << shift == d
    return lax.shift_right_logical(x, shift)


def _rms_norm(x, g):
    return x * lax.rsqrt(jnp.mean(x * x, axis=-1, keepdims=True) + RMS_EPS) * g


def _softplus(x):
    return jnp.maximum(x, 0.0) + jnp.log(1.0 + jnp.exp(-jnp.abs(x)))


def _sigmoid(x):
    return 1.0 / (1.0 + jnp.exp(-x))


def _order_masks(n, reverse, copies=1):
    row = lax.broadcasted_iota(jnp.int32, (n, copies * n), 0)
    col = lax.broadcasted_iota(jnp.int32, (n, copies * n), 1)
    if copies > 1:
        assert n & (n - 1) == 0
        col = col & (n - 1)
    if reverse:
        return col >= row, col > row
    return col <= row, col < row


def _sub_head_masks(dtype):
    lane = lax.broadcasted_iota(jnp.int32, (1, HALF), 1)
    return [((lane >= j * HEAD_DIM) & (lane < (j + 1) * HEAD_DIM)).astype(dtype)
            for j in range(HEADS_PER_HALF)]


def _lanes_of(half):
    return slice(half * HALF, (half + 1) * HALF)


def _units_and_chains(nb):
    units = [(b, half) for b in range(nb) for half in range(N_HALVES)]
    chains = [(b, half, j) for (b, half) in units for j in range(HEADS_PER_HALF)]
    return units, chains


def _join_halves(per_unit, b):
    return jnp.concatenate([per_unit[(b, half)] for half in range(N_HALVES)], axis=1)


def _params(*sem):
    return pltpu.CompilerParams(dimension_semantics=sem, vmem_limit_bytes=VMEM_LIMIT_BYTES)


def _ffn_body(x_ref, gin_ref, gout_ref, wg_ref, wu_ref, wd_ref, o_ref, acc_ref):
    x = x_ref[...]
    h = _rms_norm(x, gin_ref[...]).astype(BF16)
    for c in range(D_FF // FFN_CHUNK):
        lo = c * FFN_CHUNK
        gate = jnp.dot(h, wg_ref[:, lo:lo + FFN_CHUNK], preferred_element_type=F32)
        up = jnp.dot(h, wu_ref[:, lo:lo + FFN_CHUNK], preferred_element_type=F32)
        act = (gate * _sigmoid(gate) * up).astype(BF16)
        part = jnp.dot(act, wd_ref[lo:lo + FFN_CHUNK, :], preferred_element_type=F32)
        if c == 0:
            acc_ref[...] = part
        else:
            acc_ref[...] += part
    o_ref[...] = x + 0.5 * _rms_norm(acc_ref[...], gout_ref[...])


def _ffn(x, g_in, g_out, weights, layer, which):
    w_gate, w_up, w_down = weights
    n = x.shape[0]
    tm = min(ROW_TILE, n)
    const = lambda i: (0, 0)
    pick = lambda i: (layer, which, 0, 0)
    return pl.pallas_call(
        _ffn_body,
        grid=(n // tm,),
        in_specs=[
            pl.BlockSpec((tm, D_MODEL), lambda i: (i, 0)),
            pl.BlockSpec((1, D_MODEL), const),
            pl.BlockSpec((1, D_MODEL), const),
            pl.BlockSpec((None, None, D_MODEL, D_FF), pick),
            pl.BlockSpec((None, None, D_MODEL, D_FF), pick),
            pl.BlockSpec((None, None, D_FF, D_MODEL), pick),
        ],
        out_specs=pl.BlockSpec((tm, D_MODEL), lambda i: (i, 0)),
        out_shape=jax.ShapeDtypeStruct((n, D_MODEL), F32),
        scratch_shapes=[pltpu.VMEM((tm, D_MODEL), F32)],
        compiler_params=_params("parallel"),
        name="ffn",
    )(x, g_in, g_out, w_gate, w_up, w_down)


def _inproj_body(x_ref, g_ref, w_ref, z_ref):
    h = _rms_norm(x_ref[...], g_ref[...]).astype(BF16)
    z_ref[...] = jnp.dot(h, w_ref[...], preferred_element_type=F32)


def _inproj(x, g, w):
    n = x.shape[0]
    tm = min(ROW_TILE, n)
    return pl.pallas_call(
        _inproj_body,
        grid=(n // tm,),
        in_specs=[
            pl.BlockSpec((tm, D_MODEL), lambda i: (i, 0)),
            pl.BlockSpec((1, D_MODEL), lambda i: (0, 0)),
            pl.BlockSpec((D_MODEL, Z_COLS), lambda i: (0, 0)),
        ],
        out_specs=pl.BlockSpec((tm, Z_COLS), lambda i: (i, 0)),
        out_shape=jax.ShapeDtypeStruct((n, Z_COLS), F32),
        compiler_params=_params("parallel"),
        name="inproj",
    )(x, g, w)


def _chunk_index(nc, reverse):
    if reverse:
        return lambda c: nc - 1 - c
    return lambda c: c


def _seq_block(batch, rows, width, nc, reverse, col):
    cidx = _chunk_index(nc, reverse)
    return pl.BlockSpec((batch, rows, width), lambda c: (0, cidx(c), col))


def _const_block(shape):
    return pl.BlockSpec(shape, lambda c: (0,) * len(shape))


def _ret_section(in_refs, o_ref, st_ref, rows, *, reverse, nb):
    if reverse:
        lgs_ref, q_ref, k_ref, v_ref, cos_ref, sin_ref, lgv_ref, bdh_ref, g_ref, fwd_ref = in_refs
    else:
        lgs_ref, q_ref, k_ref, v_ref, cos_ref, sin_ref, lgv_ref, bdh_ref = in_refs
    n = RET_CHUNK
    lane = lax.broadcasted_iota(jnp.int32, (1, GROUP), 1)
    first_half = (lane % HEAD_DIM) < (HEAD_DIM // 2)
    cos = cos_ref[rows, :]
    sin = sin_ref[rows, :]

    def rot(t):
        swapped = jnp.where(first_half, pltpu.roll(t, GROUP - HEAD_DIM // 2, 1),
                            pltpu.roll(t, HEAD_DIM // 2, 1))
        return t * cos + swapped * sin

    lg = lgv_ref[...]
    ti = lax.broadcasted_iota(jnp.int32, (n, 1), 0).astype(F32)
    if reverse:
        q_exp, k_exp = n - ti, ti
    else:
        q_exp, k_exp = ti + 1.0, n - 1.0 - ti
    q_dec = jnp.exp(q_exp * lg)
    k_dec = jnp.exp(k_exp * lg)
    c_dec = jnp.exp(float(n) * lg)
    row = lax.broadcasted_iota(jnp.int32, (n, n), 0)
    col = lax.broadcasted_iota(jnp.int32, (n, n), 1)
    if reverse:
        rel, msk = (col - row).astype(F32), col > row
    else:
        rel, msk = (row - col).astype(F32), col <= row
    intra = [jnp.where(msk, jnp.exp(jnp.where(msk, rel, 0.0) * lgs_ref[h]), 0.0) for h in range(N_HEADS)]
    sub_masks = _sub_head_masks(BF16)
    units, chains = _units_and_chains(nb)

    q = [(rot(q_ref[b, rows, :]) * HEAD_DIM ** -0.5).astype(BF16) for b in range(nb)]
    k_rot = [rot(k_ref[b, rows, :]) for b in range(nb)]
    k = [t.astype(BF16) for t in k_rot]
    k_scaled = [(t * k_dec).astype(BF16) for t in k_rot]
    v = [v_ref[b, rows, :].astype(BF16) for b in range(nb)]
    yield
    state = {u: st_ref[u[0], u[1]] for u in units}
    inter = {(b, h): _mm(q[b][:, _lanes_of(h)], state[(b, h)]) for (b, h) in units}
    yield
    s = {(b, h, j): _mm_nt(q[b][:, _lanes_of(h)] * sub_masks[j], k[b][:, _lanes_of(h)])
         for (b, h, j) in chains}
    yield
    p = {(b, h, j): (s[(b, h, j)] * intra[h * HEADS_PER_HALF + j]).astype(BF16) for (b, h, j) in chains}
    intra_out = {
        (b, h): _mm(jnp.concatenate([p[(b, h, j)] for j in range(HEADS_PER_HALF)], axis=1),
                    jnp.concatenate([v[b][:, _lanes_of(h)] * sub_masks[j] for j in range(HEADS_PER_HALF)],
                                    axis=0))
        for (b, h) in units}
    yield
    for (b, h) in units:
        sl = _lanes_of(h)
        st_ref[b, h] = c_dec[:, sl] * state[(b, h)] + bdh_ref[...] * _mm_tn(k_scaled[b][:, sl], v[b][:, sl])
    yield
    outs = [q_dec * _join_halves(inter, b) + _join_halves(intra_out, b) for b in range(nb)]
    if reverse:
        tot = jnp.concatenate([fwd_ref[b, rows, :] + outs[b] for b in range(nb)], axis=0)
        normed = tot * lax.rsqrt(_head_sum(tot * tot, bdh_ref[...]) * (1.0 / HEAD_DIM) + RMS_EPS)
        for b in range(nb):
            g = g_ref[b, rows, :]
            o_ref[b, rows, :] = (g * _sigmoid(g) * _rows_of(normed, b, n)).astype(o_ref.dtype)
    else:
        for b in range(nb):
            o_ref[b, rows, :] = outs[b]


def _ret_operands(z3, cos_t, sin_t, log_gamma, bdh, rev, fwd):
    batch, seq, _ = z3.shape
    n = RET_CHUNK * RET_SUBCHUNKS
    nc = seq // n
    blk = lambda col: _seq_block(batch, n, GROUP, nc, rev, col)
    cidx = _chunk_index(nc, rev)
    tab = pl.BlockSpec((n, GROUP), lambda c: (cidx(c), 0))
    d = int(rev)
    lgv = jnp.repeat(log_gamma, HEAD_DIM, axis=1)
    specs = [pl.BlockSpec(memory_space=pltpu.SMEM), blk(ZB_Q), blk(ZB_K), blk(ZB_V), tab, tab,
             _const_block((1, GROUP)), _const_block((HALF, HALF))]
    operands = [log_gamma[d], z3, z3, z3, cos_t, sin_t, lgv[d:d + 1], bdh]
    if rev:
        specs += [blk(ZB_G), blk(0)]
        operands += [z3, fwd]
    return specs, operands


def _mlstm_section(in_refs, o_ref, c_ref, n_ref, m_ref, rows, *, reverse, nb):
    if reverse:
        (q_ref, k_ref, v_ref, gates_ref, ib_ref, fb_ref, ed_ref, bdh_ref, o_gate_ref, fwd_ref,
         nw_ref) = in_refs
    else:
        q_ref, k_ref, v_ref, gates_ref, ib_ref, fb_ref, ed_ref, bdh_ref = in_refs
    direction = int(reverse)
    n = RET_CHUNK
    incl, _ = _order_masks(n, reverse)
    sub_bf = _sub_head_masks(BF16)
    sub_f32 = _sub_head_masks(F32)
    units, chains = _units_and_chains(nb)
    ed = ed_ref[...]
    last = 0 if reverse else n - 1

    qf = [q_ref[b, rows, :] * HEAD_DIM ** -0.5 for b in range(nb)]
    q = [t.astype(BF16) for t in qf]
    kf = [k_ref[b, rows, :] for b in range(nb)]
    k = [t.astype(BF16) for t in kf]
    v = [v_ref[b, rows, :].astype(BF16) for b in range(nb)]
    gi = [pltpu.roll(gates_ref[b, rows, :] + ib_ref[...], GATE_F_LANE - GATE_I_LANE, 1) for b in range(nb)]
    lf = [-_softplus(-(gates_ref[b, rows, :] + fb_ref[...])) for b in range(nb)]
    yield
    bcum_lanes = _sel_mm(incl, jnp.concatenate(lf, axis=1))
    bcum = [bcum_lanes[:, b * LANES_V7X:(b + 1) * LANES_V7X] for b in range(nb)]
    yield
    row_terms = [(gi[b] - bcum[b]).T for b in range(nb)]
    li_all = _mm_sel(jnp.concatenate(gi, axis=0), ed)
    yield
    bcum_all = _mm_sel(jnp.concatenate(bcum, axis=0), ed)
    li_x = [_rows_of(li_all, b, n) for b in range(nb)]
    bcum_x = [_rows_of(bcum_all, b, n) for b in range(nb)]
    m_x = [m_ref[b] for b in range(nb)]
    n_x = [n_ref[b] for b in range(nb)]
    yield
    head_of = lambda c: c[1] * HEADS_PER_HALF + c[2]
    gate_lane = lambda c: GATE_F_LANE + direction * N_HEADS + head_of(c)
    colv = {c: bcum[c[0]][:, gate_lane(c):gate_lane(c) + 1] for c in chains}
    col_b = {c: jnp.broadcast_to(colv[c], (n, n)) for c in chains}
    yield
    d_log = {c: jnp.where(incl, col_b[c] + row_terms[c[0]][gate_lane(c):gate_lane(c) + 1, :], -jnp.inf)
             for c in chains}
    row_max = {c: jnp.max(d_log[c], axis=1, keepdims=True) for c in chains}
    yield
    mt = {c: jnp.maximum(colv[c] + m_x[c[0]][:, head_of(c) * HEAD_DIM:head_of(c) * HEAD_DIM + 1], row_max[c])
          for c in chains}
    mt_b = {c: jnp.broadcast_to(mt[c], (n, n)) for c in chains}
    yield
    dm = {c: jnp.exp(d_log[c] - mt_b[c]) for c in chains}
    cmat = {u: c_ref[u[0], u[1]] for u in units}
    s = {(b, h, j): (_mm_nt(q[b][:, _lanes_of(h)] * sub_bf[j], k[b][:, _lanes_of(h)]) * dm[(b, h, j)]
                     ).astype(BF16) for (b, h, j) in chains}
    yield
    ones_rows = _div_pow2(lax.broadcasted_iota(jnp.int32, (HEADS_PER_HALF * n, HALF), 0), n)
    ones_lanes = _div_pow2(lax.broadcasted_iota(jnp.int32, (HEADS_PER_HALF * n, HALF), 1), HEAD_DIM)
    head_ones = (ones_rows == ones_lanes).astype(BF16)
    num_u, den_u = {}, {}
    for (b, h) in units:
        v_h = v[b][:, _lanes_of(h)]
        rhs = jnp.concatenate([jnp.concatenate([v_h * sub_bf[j] for j in range(HEADS_PER_HALF)], axis=0),
                               head_ones], axis=1)
        nd = _mm(jnp.concatenate([s[(b, h, j)] for j in range(HEADS_PER_HALF)], axis=1), rhs)
        num_u[(b, h)] = nd[:, 0:HALF]
        den_u[(b, h)] = nd[:, HALF:2 * HALF]
    yield
    inter_q = {(b, h): _mm(q[b][:, _lanes_of(h)], cmat[(b, h)]) for (b, h) in units}
    q_dot_n = _head_sum(jnp.concatenate([qf[b] * n_x[b] for b in range(nb)], axis=0), bdh_ref[...])
    mt_u = {(b, h): mt[(b, h, 0)] * sub_f32[0] + mt[(b, h, 1)] * sub_f32[1] for (b, h) in units}
    yield

    outs = []
    for b in range(nb):
        mt_x = _join_halves(mt_u, b)
        inter_w = jnp.exp(bcum_x[b] + m_x[b] - mt_x)
        num = _join_halves(num_u, b) + inter_w * _join_halves(inter_q, b)
        den = _join_halves(den_u, b) + inter_w * _rows_of(q_dot_n, b, n)
        outs.append(num / jnp.maximum(jnp.abs(den), jnp.exp(-mt_x)))
        b_end = bcum_x[b][last:last + 1, :]
        w_log = b_end - bcum_x[b] + li_x[b]
        m_new = jnp.maximum(b_end + m_x[b], jnp.max(w_log, axis=0, keepdims=True))
        old_w = jnp.exp(b_end + m_x[b] - m_new)
        new_w = jnp.exp(w_log - m_new)
        k_w = (kf[b] * new_w).astype(BF16)
        for h in range(N_HALVES):
            sl = _lanes_of(h)
            c_ref[b, h] = old_w[:, sl] * cmat[(b, h)] + bdh_ref[...] * _mm_tn(k_w[:, sl], v[b][:, sl])
        n_ref[b] = old_w * n_x[b] + jnp.sum(new_w * kf[b], axis=0, keepdims=True)
        m_ref[b] = m_new
    if reverse:
        tot = jnp.concatenate([fwd_ref[b, rows, :] + outs[b] for b in range(nb)], axis=0)
        normed = tot * lax.rsqrt(_head_sum(tot * tot, bdh_ref[...]) * (1.0 / HEAD_DIM) + RMS_EPS)
        for b in range(nb):
            o_ref[b, rows, :] = (_sigmoid(o_gate_ref[b, rows, :]) * _rows_of(normed, b, n)
                                 * nw_ref[...]).astype(o_ref.dtype)
    else:
        for b in range(nb):
            o_ref[b, rows, :] = outs[b]


def _mlstm_operands(z3, i_bias, f_bias, norm_w, expand, bdh, rev, fwd):
    batch, seq, _ = z3.shape
    n = RET_CHUNK * RET_SUBCHUNKS
    nc = seq // n
    blk = lambda col: _seq_block(batch, n, GROUP, nc, rev, col)
    specs = [blk(ZD_Q), blk(ZD_K), blk(ZD_V), _seq_block(batch, n, LANES_V7X, nc, rev, Z_GATES_128),
             _const_block((1, LANES_V7X)), _const_block((1, LANES_V7X)),
             _const_block((LANES_V7X, GROUP)), _const_block((HALF, HALF))]
    operands = [z3, z3, z3, z3, i_bias, f_bias, expand[int(rev)], bdh]
    if rev:
        specs += [blk(ZD_O), blk(0), _const_block((1, GROUP))]
        operands += [z3, fwd, norm_w]
    return specs, operands


def _in_sequence(sections):
    for section in sections:
        yield from section


def _spaced(section, gap):
    while True:
        try:
            next(section)
        except StopIteration as stop:
            return stop.value
        yield
        yield from _pause(gap)


def _ret_mlstm_body(*refs, reverse, nb, n_ret_in):
    n_in = len(refs) - 6
    ret_o, mlstm_o, st_ref, c_ref, n_ref, m_ref = refs[n_in:]

    @pl.when(pl.program_id(0) == 0)
    def _():
        for ref in (st_ref, c_ref, n_ref, m_ref):
            ref[...] = jnp.zeros_like(ref)

    order = list(range(RET_SUBCHUNKS))
    if reverse:
        order.reverse()
    rows = lambda sub: slice(sub * RET_CHUNK, (sub + 1) * RET_CHUNK)
    _trace_interleaved(
        _in_sequence([_mlstm_section(refs[n_ret_in:n_in], mlstm_o, c_ref, n_ref, m_ref, rows(sub),
                                     reverse=reverse, nb=nb) for sub in order]),
        _in_sequence([_spaced(_ret_section(refs[:n_ret_in], ret_o, st_ref, rows(sub), reverse=reverse, nb=nb),
                              RET_STAGE_GAP) for sub in order]))


def _ret_mlstm(z3, cos_t, sin_t, log_gamma, i_bias, f_bias, norm_w, expand, bdh):
    batch, seq, _ = z3.shape
    n = RET_CHUNK * RET_SUBCHUNKS
    nc = seq // n
    scratch = [pltpu.VMEM((batch, N_HALVES, HALF, HALF), F32), pltpu.VMEM((batch, N_HALVES, HALF, HALF), F32),
               pltpu.VMEM((batch, 1, GROUP), F32), pltpu.VMEM((batch, 1, GROUP), F32)]
    partial_out = (None, None)
    for rev in (False, True):
        ret_specs, ret_ops = _ret_operands(z3, cos_t, sin_t, log_gamma, bdh, rev, partial_out[0])
        ml_specs, ml_ops = _mlstm_operands(z3, i_bias, f_bias, norm_w, expand, bdh, rev, partial_out[1])
        out_blk = _seq_block(batch, n, GROUP, nc, rev, 0)
        out_shape = jax.ShapeDtypeStruct((batch, seq, GROUP), BF16 if rev else F32)
        partial_out = pl.pallas_call(
            functools.partial(_ret_mlstm_body, reverse=rev, nb=batch, n_ret_in=len(ret_specs)),
            grid=(nc,), in_specs=ret_specs + ml_specs, out_specs=[out_blk, out_blk],
            out_shape=[out_shape, out_shape], scratch_shapes=scratch,
            compiler_params=_params("arbitrary"), name="ret_mlstm_bwd" if rev else "ret_mlstm_fwd",
        )(*ret_ops, *ml_ops)
    return partial_out


def _trace_all(gen):
    try:
        while True:
            next(gen)
    except StopIteration as stop:
        return stop.value


def _pause(stages):
    for _ in range(stages):
        yield


def _interleaved(main, side):
    live = [main] if side is None else [main, side]
    side_value = None
    while live:
        for gen in list(live):
            try:
                next(gen)
            except StopIteration as stop:
                live.remove(gen)
                if gen is side:
                    side_value = stop.value
            yield
    return side_value


def _trace_interleaved(main, side):
    return _trace_all(_interleaved(main, side))


def _rwkv_body(*refs, reverse, n_blocks, nb):
    if reverse:
        (z_ref, zp_ref, zn_ref, mu_ref, w0_ref, w2_ref, a0_ref, a2_ref, kk_ref, ka_ref, rk_ref,
         bdh_ref, g2_ref, lnw_ref, lnb_ref, wkvf_ref, bonf_ref, o_ref, s_ref) = refs
    else:
        (z_ref, zp_ref, zn_ref, mu_ref, w0_ref, w2_ref, a0_ref, a2_ref, kk_ref, ka_ref, rk_ref,
         bdh_ref, wkv_ref, bon_ref, s_ref) = refs
    step = pl.program_id(0)
    block = (n_blocks - 1 - step) if reverse else step

    @pl.when(step == 0)
    def _():
        s_ref[...] = jnp.zeros_like(s_ref)

    n = RWKV_CHUNK
    incl2, strict2 = _order_masks(n, reverse, copies=2)
    sub_masks = _sub_head_masks(BF16)
    units, chains = _units_and_chains(nb)

    def prepare(sub):
        return _rwkv_prepare(z_ref, zp_ref, zn_ref, sub, block, n_blocks, reverse, nb, mu_ref, w0_ref, w2_ref,
                             a0_ref, a2_ref, kk_ref, ka_ref, rk_ref, bdh_ref)

    def solve(prep, sub):
        rows = slice(sub * n, (sub + 1) * n)
        operand = lambda name, b, half: prep[name][b][:, _lanes_of(half)]
        state = {u: s_ref[u[0], u[1]] for u in units}
        xo = {u: _mm_nt(operand("ar", *u), state[u]) for u in units}
        yield
        ar_m = {c: operand("ar", c[0], c[1]) * sub_masks[c[2]] for c in chains}
        v_m = {c: operand("v", c[0], c[1]) * sub_masks[c[2]] for c in chains}
        bk = {u: jnp.concatenate([operand("b_t", *u), operand("k_t", *u)], axis=0) for u in units}
        scores = {c: _mm_nt(ar_m[c], bk[c[0:2]]) for c in chains}
        yield
        low = {c: jnp.where(strict2, scores[c][0:n], 0.0).astype(BF16) for c in chains}
        out_w = {c: jnp.where(incl2, scores[c][n:2 * n], 0.0).astype(BF16) for c in chains}
        zeros_n = jnp.zeros((n, HALF), BF16)
        lak_v = {c: _mm(low[c], jnp.concatenate([zeros_n, v_m[c]], axis=0)) for c in chains}
        x = {u: xo[u][0:n] + lak_v[u + (0,)] + lak_v[u + (1,)] for u in units}
        yield
        p = {c: low[c][:, 0:n] for c in chains}
        u_c = {c: x[c[0:2]] * sub_masks[c[2]].astype(F32) for c in chains}
        u_c = {c: u_c[c] + _mm(p[c], u_c[c]) for c in chains}
        yield
        span = 1
        while 2 * span < n:
            p = {c: _mm(p[c], p[c]).astype(BF16) for c in chains}
            yield
            u_c = {c: u_c[c] + _mm(p[c], u_c[c]) for c in chains}
            yield
            span *= 2
        o_c = {c: _mm(out_w[c], jnp.concatenate([u_c[c].astype(BF16), v_m[c]], axis=0)) for c in chains}
        yield
        for u in units:
            b, half = u
            u_all = u_c[u + (0,)] + u_c[u + (1,)]
            upd = (_mm_tn(u_all, operand("b_t", b, half))
                   + _mm_tn(operand("v", b, half), operand("k_t", b, half)))
            s_ref[b, half] = prep["gam_end"][b][:, _lanes_of(half)] * (state[u] + bdh_ref[...] * upd)
        yield
        wkv_u = {u: xo[u][n:2 * n] + o_c[u + (0,)] + o_c[u + (1,)] for u in units}
        if reverse:
            bdh = bdh_ref[...]
            wkv = jnp.concatenate([wkvf_ref[b, rows, :] + _join_halves(wkv_u, b) for b in range(nb)], axis=0)
            cen = wkv - _head_sum(wkv, bdh) * (1.0 / HEAD_DIM)
            var = _head_sum(cen * cen, bdh) * (1.0 / HEAD_DIM)
            gate = _mm(_sigmoid(prep["lat2"]), g2_ref[...])
            out = (cen * lax.rsqrt(var + RWKV_GN_EPS) * lnw_ref[...] + lnb_ref[...] + prep["bonus"]) * gate
            for b in range(nb):
                o_ref[b, rows, :] = (_rows_of(out, b, n)
                                     + bonf_ref[b, rows, :] * _rows_of(gate, b, n)).astype(o_ref.dtype)
        else:
            for b in range(nb):
                wkv_ref[b, rows, :] = _join_halves(wkv_u, b)
                bon_ref[b, rows, :] = _rows_of(prep["bonus"], b, n)

    order = list(range(RWKV_SUBCHUNKS))
    if reverse:
        order.reverse()
    prep = _trace_all(prepare(order[0]))
    for i, sub in enumerate(order):
        following = prepare(order[i + 1]) if i + 1 < len(order) else None
        prep = _trace_interleaved(solve(prep, sub), following)


def _rwkv_prepare(z_ref, zp_ref, zn_ref, sub, block, n_blocks, reverse, nb, mu_ref, w0_ref, w2_ref, a0_ref,
                  a2_ref, kk_ref, ka_ref, rk_ref, bdh_ref):
    n = RWKV_CHUNK
    lo, hi = sub * n, (sub + 1) * n
    ridx = lax.broadcasted_iota(jnp.int32, (n, 1), 0)
    shifted = []
    for b in range(nb):
        z = z_ref[b, lo:hi, :]
        if sub == 0:
            z_prev = jnp.where(block > 0, zp_ref[b, SUBLANES_V7X - 1:SUBLANES_V7X, :], 0.0)
        else:
            z_prev = z_ref[b, lo - 1:lo, :]
        if sub == RWKV_SUBCHUNKS - 1:
            z_next = jnp.where(block < n_blocks - 1, zn_ref[b, 0:1, :], 0.0)
        else:
            z_next = z_ref[b, hi:hi + 1, :]
        z_m1 = jnp.where(ridx == 0, z_prev, pltpu.roll(z, 1, 0))
        z_p1 = jnp.where(ridx == n - 1, z_next, pltpu.roll(z, n - 1, 0))
        shifted.append(z + mu_ref[...] * (0.5 * (z_m1 + z_p1) - z))
    zs = jnp.concatenate(shifted, axis=0)
    bdh = bdh_ref[...]
    r = zs[:, 0:GROUP]
    k = zs[:, GROUP:2 * GROUP]
    v = zs[:, 2 * GROUP:3 * GROUP]
    lat1 = zs[:, 3 * GROUP:3 * GROUP + LANES_V7X]
    lat2 = zs[:, 3 * GROUP + LANES_V7X:4 * GROUP]
    yield from _pause(RWKV_PREP_SPACING)
    w_pre = w0_ref[...] + _mm(jnp.tanh(lat1), w2_ref[...])
    log_decay = -jnp.exp(-_softplus(-w_pre) - 0.5)
    a = _sigmoid(a0_ref[...] + _mm(lat1, a2_ref[...]))
    kk_raw = k * kk_ref[...]
    yield from _pause(RWKV_PREP_SPACING)
    kk = kk_raw / jnp.maximum(jnp.sqrt(_head_sum(kk_raw * kk_raw, bdh)), 1e-12)
    kd = k * (1.0 + (a - 1.0) * ka_ref[...])
    yield from _pause(RWKV_PREP_SPACING)
    bonus = _head_sum(r * kd * rk_ref[...], bdh) * v
    yield from _pause(RWKV_PREP_SPACING)
    rows = nb * n
    row = lax.broadcasted_iota(jnp.int32, (rows, rows), 0)
    col = lax.broadcasted_iota(jnp.int32, (rows, rows), 1)
    same_seq = _div_pow2(row, n) == _div_pow2(col, n)
    before = (col >= row) if reverse else (col <= row)
    cum_incl = _sel_mm(same_seq & before, log_decay, terms=2)
    cum_excl = cum_incl - log_decay
    last = 0 if reverse else n - 1
    a_t = (-kk * jnp.exp(cum_excl)).astype(BF16)
    r_t = (r * jnp.exp(cum_incl)).astype(BF16)
    inv_g = jnp.exp(-cum_incl)
    b_t = (kk * a * inv_g).astype(BF16)
    k_t = (kd * inv_g).astype(BF16)
    vb = v.astype(BF16)
    per_seq = lambda t: [_rows_of(t, b, n) for b in range(nb)]
    return dict(ar=[jnp.concatenate([_rows_of(a_t, b, n), _rows_of(r_t, b, n)], axis=0) for b in range(nb)],
                b_t=per_seq(b_t), k_t=per_seq(k_t), v=per_seq(vb),
                gam_end=[jnp.exp(cum_incl[b * n + last:b * n + last + 1, :]) for b in range(nb)],
                bonus=bonus, lat2=lat2)


def _rwkv(z3, p, bdh):
    batch, seq, _ = z3.shape
    rows = RWKV_CHUNK * RWKV_SUBCHUNKS
    n_blocks = seq // rows
    rows8 = rows // SUBLANES_V7X
    last8 = seq // SUBLANES_V7X - 1
    wide = 4 * GROUP

    def halo(rev, offset):
        bidx = _chunk_index(n_blocks, rev)
        return pl.BlockSpec((batch, SUBLANES_V7X, wide), lambda c: (
            0, jnp.clip((bidx(c) + offset) * rows8 - (1 - offset), 0, last8), ZC_BLOCK_1024))

    vec = _const_block((1, GROUP))
    lora = _const_block((LANES_V7X, GROUP))
    out_blk = lambda rev: _seq_block(batch, rows, GROUP, n_blocks, rev, 0)
    out_shape = jax.ShapeDtypeStruct((batch, seq, GROUP), F32)

    def specs(rev):
        return [_seq_block(batch, rows, wide, n_blocks, rev, ZC_BLOCK_1024), halo(rev, 0), halo(rev, 1),
                _const_block((1, wide)), vec, lora, vec, lora, vec, vec, vec, _const_block((HALF, HALF))]

    common = dict(grid=(n_blocks,), scratch_shapes=[pltpu.VMEM((batch, N_HALVES, HALF, HALF), F32)],
                  compiler_params=_params("arbitrary"))
    wkv_f, bon_f = pl.pallas_call(
        functools.partial(_rwkv_body, reverse=False, n_blocks=n_blocks, nb=batch),
        in_specs=specs(False), out_specs=[out_blk(False), out_blk(False)],
        out_shape=[out_shape, out_shape], name="rwkv_fwd", **common,
    )(z3, z3, z3, p["mu"], p["w0"][0], p["w2"][0], p["a0"][0], p["a2"][0], p["k_k"], p["k_a"], p["r_k"], bdh)
    return pl.pallas_call(
        functools.partial(_rwkv_body, reverse=True, n_blocks=n_blocks, nb=batch),
        in_specs=specs(True) + [lora, vec, vec, out_blk(True), out_blk(True)],
        out_specs=out_blk(True), out_shape=jax.ShapeDtypeStruct((batch, seq, GROUP), BF16),
        name="rwkv_bwd", **common,
    )(z3, z3, z3, p["mu"], p["w0"][1], p["w2"][1], p["a0"][1], p["a2"][1], p["k_k"], p["k_a"], p["r_k"], bdh,
      p["g2"], p["ln_w"], p["ln_b"], wkv_f, bon_f)


def _outproj_body(x_ref, cb_ref, cc_ref, ch_ref, ccp_ref, chp_ref, ccn_ref, chn_ref, yb_ref, yc_ref, yd_ref,
                  cw_ref, cbias_ref, w_ref, g_ref, o_ref, *, tiles_per_seq):
    tm = x_ref.shape[0]
    t = pl.program_id(0) % tiles_per_seq
    u = cc_ref[...] * ch_ref[...]
    last8 = SUBLANES_V7X - 1
    u_prev = jnp.where(t > 0, ccp_ref[last8:last8 + 1, :] * chp_ref[last8:last8 + 1, :], 0.0)
    u_next = jnp.where(t < tiles_per_seq - 1, ccn_ref[0:1, :] * chn_ref[0:1, :], 0.0)
    ridx = lax.broadcasted_iota(jnp.int32, (tm, 1), 0)
    u_m1 = jnp.where(ridx == 0, u_prev, pltpu.roll(u, 1, 0))
    u_p1 = jnp.where(ridx == tm - 1, u_next, pltpu.roll(u, tm - 1, 0))
    cw = cw_ref[...]
    conv = cw[0:1, :] * u_m1 + cw[1:2, :] * u + cw[2:3, :] * u_p1 + cbias_ref[...]
    y_conv = cb_ref[...] * conv
    y = (_mm(y_conv, w_ref[0:GROUP, :]) + _mm(yb_ref[...], w_ref[GROUP:2 * GROUP, :])
         + _mm(yc_ref[...], w_ref[2 * GROUP:3 * GROUP, :]) + _mm(yd_ref[...], w_ref[3 * GROUP:4 * GROUP, :]))
    o_ref[...] = x_ref[...] + _rms_norm(y, g_ref[...])


def _outproj(x, z, y_ret, y_rwkv, y_mlstm, conv_w8, conv_b, w_out, layer, g, seq):
    n = x.shape[0]
    tm = min(OUTPROJ_ROW_TILE, seq)
    tiles_per_seq = seq // tm
    rows8 = tm // SUBLANES_V7X
    last8 = n // SUBLANES_V7X - 1
    zblk = lambda col: pl.BlockSpec((tm, GROUP), lambda i: (i, col))
    prev = lambda col: pl.BlockSpec((SUBLANES_V7X, GROUP), lambda i: (jnp.maximum(i * rows8 - 1, 0), col))
    nxt = lambda col: pl.BlockSpec((SUBLANES_V7X, GROUP), lambda i: (jnp.minimum((i + 1) * rows8, last8), col))
    yblk = pl.BlockSpec((tm, GROUP), lambda i: (i, 0))
    const = lambda i: (0, 0)
    return pl.pallas_call(
        functools.partial(_outproj_body, tiles_per_seq=tiles_per_seq),
        grid=(n // tm,),
        in_specs=[pl.BlockSpec((tm, D_MODEL), lambda i: (i, 0)),
                  zblk(ZA_B), zblk(ZA_C), zblk(ZA_H), prev(ZA_C), prev(ZA_H), nxt(ZA_C), nxt(ZA_H),
                  yblk, yblk, yblk,
                  pl.BlockSpec((SUBLANES_V7X, GROUP), const), pl.BlockSpec((1, GROUP), const),
                  pl.BlockSpec((None, D_MODEL, D_MODEL), lambda i: (layer, 0, 0)),
                  pl.BlockSpec((1, D_MODEL), const)],
        out_specs=pl.BlockSpec((tm, D_MODEL), lambda i: (i, 0)),
        out_shape=jax.ShapeDtypeStruct((n, D_MODEL), F32),
        compiler_params=_params("parallel"),
        name="outproj",
    )(x, z, z, z, z, z, z, z, y_ret, y_rwkv, y_mlstm, conv_w8, conv_b, w_out, g)


def _relayout_w_in(w):
    pad = lambda c: jnp.zeros((D_MODEL, c), w.dtype)
    a, b = w[:, 0:768], w[:, 768:1792]
    c = w[:, 1792:1792 + RWKV_COLS]
    d0 = 1792 + RWKV_COLS
    d = w[:, d0:d0 + 4 * GROUP]
    gi = w[:, d0 + 4 * GROUP:d0 + 4 * GROUP + 2 * N_HEADS]
    gf = w[:, d0 + 4 * GROUP + 2 * N_HEADS:d0 + 4 * GROUP + 4 * N_HEADS]
    c_pad = 4 * GROUP - RWKV_COLS - 4 * N_HEADS
    assert RWKV_COLS - (4 * GROUP - LANES_V7X) == GATE_I_LANE and GATE_F_LANE == GATE_I_LANE + 2 * N_HEADS
    return jnp.concatenate([b, c, gi, gf, pad(c_pad), a, d], axis=1).astype(BF16)


def _block_diag_ones():
    idx = np.arange(HALF) // HEAD_DIM
    return jnp.asarray((idx[:, None] == idx[None, :]).astype(np.float32))


def _gate_expand():
    j = np.arange(LANES_V7X)[:, None]
    head = np.arange(GROUP)[None, :] // HEAD_DIM
    return jnp.asarray(np.stack([(j == GATE_F_LANE + d * N_HEADS + head) for d in range(2)]).astype(np.float32))


def _rope_tables(seq):
    inv = 10000.0 ** (-jnp.arange(0, HEAD_DIM, 2, dtype=F32) / HEAD_DIM)
    ang = jnp.arange(seq, dtype=F32)[:, None] * inv[None, :]
    cos, sin = jnp.cos(ang), jnp.sin(ang)
    cos_t = jnp.tile(jnp.concatenate([cos, cos], axis=1), (1, N_HEADS))
    sin_t = jnp.tile(jnp.concatenate([-sin, sin], axis=1), (1, N_HEADS))
    return cos_t, sin_t


def _lane_pad(v, width, start=0):
    v = v.reshape(1, -1)
    return jnp.pad(v, ((0, 0), (start, width - start - v.shape[1])))


def _rwkv_params(mu, w0, w2, a0, a2, g2, k_k, k_a, r_k, ln_w, ln_b):
    rank_w, rank_a, rank_g = w2.shape[1], a2.shape[1], g2.shape[0]
    w2p = jnp.zeros((2, LANES_V7X, GROUP), F32)
    a2p = jnp.zeros((2, LANES_V7X, GROUP), F32)
    for d in range(2):
        w2p = w2p.at[d, d * rank_w:(d + 1) * rank_w].set(w2[d])
        a2p = a2p.at[d, 2 * rank_w + d * rank_a:2 * rank_w + (d + 1) * rank_a].set(a2[d])
    g2p = jnp.zeros((LANES_V7X, GROUP), F32).at[0:rank_g].set(g2)
    return dict(mu=_lane_pad(mu, 4 * GROUP), w0=w0.reshape(2, 1, GROUP), w2=w2p.astype(BF16),
                a0=a0.reshape(2, 1, GROUP), a2=a2p.astype(BF16), g2=g2p.astype(BF16),
                k_k=k_k.reshape(1, GROUP), k_a=k_a.reshape(1, GROUP), r_k=r_k.reshape(1, GROUP),
                ln_w=ln_w.reshape(1, GROUP), ln_b=ln_b.reshape(1, GROUP))


def kernel(x, norm_g, ffn_w_gate, ffn_w_up, ffn_w_down, w_in, w_out, conv_w, conv_b, ret_decay_logit, rwkv_mu, rwkv_w0, rwkv_w2, rwkv_a0, rwkv_a2, rwkv_g2, rwkv_k_k, rwkv_k_a, rwkv_r_k, rwkv_ln_w, rwkv_ln_b, mlstm_i_bias, mlstm_f_bias, mlstm_norm_w):
    batch, seq, _ = x.shape
    depth = norm_g.shape[0]
    n_tok = batch * seq
    assert seq % OUTPROJ_ROW_TILE == 0 and seq % (RET_CHUNK * RET_SUBCHUNKS) == 0
    assert seq % (RWKV_CHUNK * RWKV_SUBCHUNKS) == 0
    cos_t, sin_t = _rope_tables(seq)
    bdh = _block_diag_ones()
    expand = _gate_expand()
    xf = x.reshape(n_tok, D_MODEL)
    ffn_w = (ffn_w_gate.astype(BF16), ffn_w_up.astype(BF16), ffn_w_down.astype(BF16))
    w_out_bf = w_out.astype(BF16)
    for l in range(depth):
        g = norm_g[l].reshape(6, 1, D_MODEL)
        xf = _ffn(xf, g[0], g[1], ffn_w, l, 0)
        z = _inproj(xf, g[2], _relayout_w_in(w_in[l]))
        z3 = z.reshape(batch, seq, Z_COLS)
        log_gamma = jax.nn.log_sigmoid(ret_decay_logit[l].astype(F32))
        y_ret, y_mlstm = _ret_mlstm(
            z3, cos_t, sin_t, log_gamma, _lane_pad(mlstm_i_bias[l], LANES_V7X, GATE_I_LANE),
            _lane_pad(mlstm_f_bias[l], LANES_V7X, GATE_F_LANE), mlstm_norm_w[l].reshape(1, GROUP), expand, bdh)
        y_rwkv = _rwkv(z3, _rwkv_params(
            rwkv_mu[l], rwkv_w0[l], rwkv_w2[l], rwkv_a0[l], rwkv_a2[l], rwkv_g2[l], rwkv_k_k[l],
            rwkv_k_a[l], rwkv_r_k[l], rwkv_ln_w[l], rwkv_ln_b[l]), bdh)
        conv_w8 = jnp.pad(conv_w[l], ((0, SUBLANES_V7X - conv_w.shape[1]), (0, 0)))
        xf = _outproj(xf, z, y_ret.reshape(n_tok, GROUP), y_rwkv.reshape(n_tok, GROUP),
                      y_mlstm.reshape(n_tok, GROUP), conv_w8, conv_b[l].reshape(1, GROUP),
                      w_out_bf, l, g[3], seq)
        xf = _ffn(xf, g[4], g[5], ffn_w, l, 1)
    return xf.reshape(batch, seq, D_MODEL)
```

```python
import functools

import numpy as np
import jax
import jax.numpy as jnp
from jax import lax
from jax.experimental import pallas as pl
from jax.experimental.pallas import tpu as pltpu

F32 = jnp.float32
BF16 = jnp.bfloat16

D_MODEL = 1024
GROUP = 256
N_HEADS = 4
HEAD_DIM = 64
D_FF = 2816
RWKV_COLS = 960
RMS_EPS = 1e-6
RWKV_GN_EPS = 64e-5

LANES_V7X = 128
SUBLANES_V7X = 8
VMEM_LIMIT_BYTES = 56 * 1024 * 1024

ROW_TILE = 512
OUTPROJ_ROW_TILE = 1024
FFN_CHUNK = 256
RET_CHUNK = 128
RET_SUBCHUNKS = 2
RET_STAGE_GAP = 1
RWKV_CHUNK = 64
RWKV_SUBCHUNKS = 4
RWKV_PREP_SPACING = 3

HALF = LANES_V7X
N_HALVES = GROUP // HALF
HEADS_PER_HALF = HALF // HEAD_DIM

Z_COLS = 3840
ZB_Q, ZB_K, ZB_V, ZB_G = 0, 1, 2, 3
ZC_BLOCK_1024 = 1
ZA_B, ZA_C, ZA_H = 8, 9, 10
ZD_Q, ZD_K, ZD_V, ZD_O = 11, 12, 13, 14
Z_GATES_128 = 15
GATE_I_LANE = 64
GATE_F_LANE = 72


def _mm(a, b):
    return jnp.dot(a.astype(BF16), b.astype(BF16), preferred_element_type=F32)


def _mm_nt(a, b):
    return lax.dot_general(a.astype(BF16), b.astype(BF16), (((1,), (1,)), ((), ())),
                           preferred_element_type=F32)


def _mm_tn(a, b):
    return lax.dot_general(a.astype(BF16), b.astype(BF16), (((0,), (0,)), ((), ())),
                           preferred_element_type=F32)


def _bf16_terms(x, terms):
    out = []
    for _ in range(terms - 1):
        part = x.astype(BF16)
        out.append(part)
        x = x - part.astype(F32)
    out.append(x.astype(BF16))
    return out


def _sel_mm(sel, x, terms=3):
    s = sel.astype(BF16)
    return sum(jnp.dot(s, part, preferred_element_type=F32) for part in _bf16_terms(x, terms))


def _mm_sel(x, sel, terms=3):
    s = sel.astype(BF16)
    return sum(jnp.dot(part, s, preferred_element_type=F32) for part in _bf16_terms(x, terms))


def _head_sum(x, bdh):
    return jnp.concatenate([_mm(x[:, _lanes_of(h)], bdh) for h in range(N_HALVES)], axis=1)


def _rows_of(stacked, b, n):
    return stacked[b * n:(b + 1) * n]


def _div_pow2(x, d):
    shift = d.bit_length() - 1
    assert 1 << shift == d
    return lax.shift_right_logical(x, shift)


def _rms_norm(x, g):
    return x * lax.rsqrt(jnp.mean(x * x, axis=-1, keepdims=True) + RMS_EPS) * g


def _softplus(x):
    return jnp.maximum(x, 0.0) + jnp.log(1.0 + jnp.exp(-jnp.abs(x)))


def _sigmoid(x):
    return 1.0 / (1.0 + jnp.exp(-x))


def _order_masks(n, reverse, copies=1):
    row = lax.broadcasted_iota(jnp.int32, (n, copies * n), 0)
    col = lax.broadcasted_iota(jnp.int32, (n, copies * n), 1)
    if copies > 1:
        assert n & (n - 1) == 0
        col = col & (n - 1)
    if reverse:
        return col >= row, col > row
    return col <= row, col < row


def _sub_head_masks(dtype):
    lane = lax.broadcasted_iota(jnp.int32, (1, HALF), 1)
    return [((lane >= j * HEAD_DIM) & (lane < (j + 1) * HEAD_DIM)).astype(dtype)
            for j in range(HEADS_PER_HALF)]


def _lanes_of(half):
    return slice(half * HALF, (half + 1) * HALF)


def _units_and_chains(nb):
    units = [(b, half) for b in range(nb) for half in range(N_HALVES)]
    chains = [(b, half, j) for (b, half) in units for j in range(HEADS_PER_HALF)]
    return units, chains


def _join_halves(per_unit, b):
    return jnp.concatenate([per_unit[(b, half)] for half in range(N_HALVES)], axis=1)


def _params(*sem):
    return pltpu.CompilerParams(dimension_semantics=sem, vmem_limit_bytes=VMEM_LIMIT_BYTES)


def _ffn_body(x_ref, gin_ref, gout_ref, wg_ref, wu_ref, wd_ref, o_ref, acc_ref):
    x = x_ref[...]
    h = _rms_norm(x, gin_ref[...]).astype(BF16)
    for c in range(D_FF // FFN_CHUNK):
        lo = c * FFN_CHUNK
        gate = jnp.dot(h, wg_ref[:, lo:lo + FFN_CHUNK], preferred_element_type=F32)
        up = jnp.dot(h, wu_ref[:, lo:lo + FFN_CHUNK], preferred_element_type=F32)
        act = (gate * _sigmoid(gate) * up).astype(BF16)
        part = jnp.dot(act, wd_ref[lo:lo + FFN_CHUNK, :], preferred_element_type=F32)
        if c == 0:
            acc_ref[...] = part
        else:
            acc_ref[...] += part
    o_ref[...] = x + 0.5 * _rms_norm(acc_ref[...], gout_ref[...])


def _ffn(x, g_in, g_out, weights, layer, which):
    w_gate, w_up, w_down = weights
    n = x.shape[0]
    tm = min(ROW_TILE, n)
    const = lambda i: (0, 0)
    pick = lambda i: (layer, which, 0, 0)
    return pl.pallas_call(
        _ffn_body,
        grid=(n // tm,),
        in_specs=[
            pl.BlockSpec((tm, D_MODEL), lambda i: (i, 0)),
            pl.BlockSpec((1, D_MODEL), const),
            pl.BlockSpec((1, D_MODEL), const),
            pl.BlockSpec((None, None, D_MODEL, D_FF), pick),
            pl.BlockSpec((None, None, D_MODEL, D_FF), pick),
            pl.BlockSpec((None, None, D_FF, D_MODEL), pick),
        ],
        out_specs=pl.BlockSpec((tm, D_MODEL), lambda i: (i, 0)),
        out_shape=jax.ShapeDtypeStruct((n, D_MODEL), F32),
        scratch_shapes=[pltpu.VMEM((tm, D_MODEL), F32)],
        compiler_params=_params("parallel"),
        name="ffn",
    )(x, g_in, g_out, w_gate, w_up, w_down)


def _inproj_body(x_ref, g_ref, w_ref, z_ref):
    h = _rms_norm(x_ref[...], g_ref[...]).astype(BF16)
    z_ref[...] = jnp.dot(h, w_ref[...], preferred_element_type=F32)


def _inproj(x, g, w):
    n = x.shape[0]
    tm = min(ROW_TILE, n)
    return pl.pallas_call(
        _inproj_body,
        grid=(n // tm,),
        in_specs=[
            pl.BlockSpec((tm, D_MODEL), lambda i: (i, 0)),
            pl.BlockSpec((1, D_MODEL), lambda i: (0, 0)),
            pl.BlockSpec((D_MODEL, Z_COLS), lambda i: (0, 0)),
        ],
        out_specs=pl.BlockSpec((tm, Z_COLS), lambda i: (i, 0)),
        out_shape=jax.ShapeDtypeStruct((n, Z_COLS), F32),
        compiler_params=_params("parallel"),
        name="inproj",
    )(x, g, w)


def _chunk_index(nc, reverse):
    if reverse:
        return lambda c: nc - 1 - c
    return lambda c: c


def _seq_block(batch, rows, width, nc, reverse, col):
    cidx = _chunk_index(nc, reverse)
    return pl.BlockSpec((batch, rows, width), lambda c: (0, cidx(c), col))


def _const_block(shape):
    return pl.BlockSpec(shape, lambda c: (0,) * len(shape))


def _ret_section(in_refs, o_ref, st_ref, rows, *, reverse, nb):
    if reverse:
        lgs_ref, q_ref, k_ref, v_ref, cos_ref, sin_ref, lgv_ref, bdh_ref, g_ref, fwd_ref = in_refs
    else:
        lgs_ref, q_ref, k_ref, v_ref, cos_ref, sin_ref, lgv_ref, bdh_ref = in_refs
    n = RET_CHUNK
    lane = lax.broadcasted_iota(jnp.int32, (1, GROUP), 1)
    first_half = (lane % HEAD_DIM) < (HEAD_DIM // 2)
    cos = cos_ref[rows, :]
    sin = sin_ref[rows, :]

    def rot(t):
        swapped = jnp.where(first_half, pltpu.roll(t, GROUP - HEAD_DIM // 2, 1),
                            pltpu.roll(t, HEAD_DIM // 2, 1))
        return t * cos + swapped * sin

    lg = lgv_ref[...]
    ti = lax.broadcasted_iota(jnp.int32, (n, 1), 0).astype(F32)
    if reverse:
        q_exp, k_exp = n - ti, ti
    else:
        q_exp, k_exp = ti + 1.0, n - 1.0 - ti
    q_dec = jnp.exp(q_exp * lg)
    k_dec = jnp.exp(k_exp * lg)
    c_dec = jnp.exp(float(n) * lg)
    row = lax.broadcasted_iota(jnp.int32, (n, n), 0)
    col = lax.broadcasted_iota(jnp.int32, (n, n), 1)
    if reverse:
        rel, msk = (col - row).astype(F32), col > row
    else:
        rel, msk = (row - col).astype(F32), col <= row
    intra = [jnp.where(msk, jnp.exp(jnp.where(msk, rel, 0.0) * lgs_ref[h]), 0.0) for h in range(N_HEADS)]
    sub_masks = _sub_head_masks(BF16)
    units, chains = _units_and_chains(nb)

    q = [(rot(q_ref[b, rows, :]) * HEAD_DIM ** -0.5).astype(BF16) for b in range(nb)]
    k_rot = [rot(k_ref[b, rows, :]) for b in range(nb)]
    k = [t.astype(BF16) for t in k_rot]
    k_scaled = [(t * k_dec).astype(BF16) for t in k_rot]
    v = [v_ref[b, rows, :].astype(BF16) for b in range(nb)]
    yield
    state = {u: st_ref[u[0], u[1]] for u in units}
    inter = {(b, h): _mm(q[b][:, _lanes_of(h)], state[(b, h)]) for (b, h) in units}
    yield
    s = {(b, h, j): _mm_nt(q[b][:, _lanes_of(h)] * sub_masks[j], k[b][:, _lanes_of(h)])
         for (b, h, j) in chains}
    yield
    p = {(b, h, j): (s[(b, h, j)] * intra[h * HEADS_PER_HALF + j]).astype(BF16) for (b, h, j) in chains}
    intra_out = {
        (b, h): _mm(jnp.concatenate([p[(b, h, j)] for j in range(HEADS_PER_HALF)], axis=1),
                    jnp.concatenate([v[b][:, _lanes_of(h)] * sub_masks[j] for j in range(HEADS_PER_HALF)],
                                    axis=0))
        for (b, h) in units}
    yield
    for (b, h) in units:
        sl = _lanes_of(h)
        st_ref[b, h] = c_dec[:, sl] * state[(b, h)] + bdh_ref[...] * _mm_tn(k_scaled[b][:, sl], v[b][:, sl])
    yield
    outs = [q_dec * _join_halves(inter, b) + _join_halves(intra_out, b) for b in range(nb)]
    if reverse:
        tot = jnp.concatenate([fwd_ref[b, rows, :] + outs[b] for b in range(nb)], axis=0)
        normed = tot * lax.rsqrt(_head_sum(tot * tot, bdh_ref[...]) * (1.0 / HEAD_DIM) + RMS_EPS)
        for b in range(nb):
            g = g_ref[b, rows, :]
            o_ref[b, rows, :] = (g * _sigmoid(g) * _rows_of(normed, b, n)).astype(o_ref.dtype)
    else:
        for b in range(nb):
            o_ref[b, rows, :] = outs[b]


def _ret_operands(z3, cos_t, sin_t, log_gamma, bdh, rev, fwd):
    batch, seq, _ = z3.shape
    n = RET_CHUNK * RET_SUBCHUNKS
    nc = seq // n
    blk = lambda col: _seq_block(batch, n, GROUP, nc, rev, col)
    cidx = _chunk_index(nc, rev)
    tab = pl.BlockSpec((n, GROUP), lambda c: (cidx(c), 0))
    d = int(rev)
    lgv = jnp.repeat(log_gamma, HEAD_DIM, axis=1)
    specs = [pl.BlockSpec(memory_space=pltpu.SMEM), blk(ZB_Q), blk(ZB_K), blk(ZB_V), tab, tab,
             _const_block((1, GROUP)), _const_block((HALF, HALF))]
    operands = [log_gamma[d], z3, z3, z3, cos_t, sin_t, lgv[d:d + 1], bdh]
    if rev:
        specs += [blk(ZB_G), blk(0)]
        operands += [z3, fwd]
    return specs, operands


def _mlstm_section(in_refs, o_ref, c_ref, n_ref, m_ref, rows, *, reverse, nb):
    if reverse:
        (q_ref, k_ref, v_ref, gates_ref, ib_ref, fb_ref, ed_ref, bdh_ref, o_gate_ref, fwd_ref,
         nw_ref) = in_refs
    else:
        q_ref, k_ref, v_ref, gates_ref, ib_ref, fb_ref, ed_ref, bdh_ref = in_refs
    direction = int(reverse)
    n = RET_CHUNK
    incl, _ = _order_masks(n, reverse)
    sub_bf = _sub_head_masks(BF16)
    sub_f32 = _sub_head_masks(F32)
    units, chains = _units_and_chains(nb)
    ed = ed_ref[...]
    last = 0 if reverse else n - 1

    qf = [q_ref[b, rows, :] * HEAD_DIM ** -0.5 for b in range(nb)]
    q = [t.astype(BF16) for t in qf]
    kf = [k_ref[b, rows, :] for b in range(nb)]
    k = [t.astype(BF16) for t in kf]
    v = [v_ref[b, rows, :].astype(BF16) for b in range(nb)]
    gi = [pltpu.roll(gates_ref[b, rows, :] + ib_ref[...], GATE_F_LANE - GATE_I_LANE, 1) for b in range(nb)]
    lf = [-_softplus(-(gates_ref[b, rows, :] + fb_ref[...])) for b in range(nb)]
    yield
    bcum_lanes = _sel_mm(incl, jnp.concatenate(lf, axis=1))
    bcum = [bcum_lanes[:, b * LANES_V7X:(b + 1) * LANES_V7X] for b in range(nb)]
    yield
    row_terms = [(gi[b] - bcum[b]).T for b in range(nb)]
    li_all = _mm_sel(jnp.concatenate(gi, axis=0), ed)
    yield
    bcum_all = _mm_sel(jnp.concatenate(bcum, axis=0), ed)
    li_x = [_rows_of(li_all, b, n) for b in range(nb)]
    bcum_x = [_rows_of(bcum_all, b, n) for b in range(nb)]
    m_x = [m_ref[b] for b in range(nb)]
    n_x = [n_ref[b] for b in range(nb)]
    yield
    head_of = lambda c: c[1] * HEADS_PER_HALF + c[2]
    gate_lane = lambda c: GATE_F_LANE + direction * N_HEADS + head_of(c)
    colv = {c: bcum[c[0]][:, gate_lane(c):gate_lane(c) + 1] for c in chains}
    col_b = {c: jnp.broadcast_to(colv[c], (n, n)) for c in chains}
    yield
    d_log = {c: jnp.where(incl, col_b[c] + row_terms[c[0]][gate_lane(c):gate_lane(c) + 1, :], -jnp.inf)
             for c in chains}
    row_max = {c: jnp.max(d_log[c], axis=1, keepdims=True) for c in chains}
    yield
    mt = {c: jnp.maximum(colv[c] + m_x[c[0]][:, head_of(c) * HEAD_DIM:head_of(c) * HEAD_DIM + 1], row_max[c])
          for c in chains}
    mt_b = {c: jnp.broadcast_to(mt[c], (n, n)) for c in chains}
    yield
    dm = {c: jnp.exp(d_log[c] - mt_b[c]) for c in chains}
    cmat = {u: c_ref[u[0], u[1]] for u in units}
    s = {(b, h, j): (_mm_nt(q[b][:, _lanes_of(h)] * sub_bf[j], k[b][:, _lanes_of(h)]) * dm[(b, h, j)]
                     ).astype(BF16) for (b, h, j) in chains}
    yield
    ones_rows = _div_pow2(lax.broadcasted_iota(jnp.int32, (HEADS_PER_HALF * n, HALF), 0), n)
    ones_lanes = _div_pow2(lax.broadcasted_iota(jnp.int32, (HEADS_PER_HALF * n, HALF), 1), HEAD_DIM)
    head_ones = (ones_rows == ones_lanes).astype(BF16)
    num_u, den_u = {}, {}
    for (b, h) in units:
        v_h = v[b][:, _lanes_of(h)]
        rhs = jnp.concatenate([jnp.concatenate([v_h * sub_bf[j] for j in range(HEADS_PER_HALF)], axis=0),
                               head_ones], axis=1)
        nd = _mm(jnp.concatenate([s[(b, h, j)] for j in range(HEADS_PER_HALF)], axis=1), rhs)
        num_u[(b, h)] = nd[:, 0:HALF]
        den_u[(b, h)] = nd[:, HALF:2 * HALF]
    yield
    inter_q = {(b, h): _mm(q[b][:, _lanes_of(h)], cmat[(b, h)]) for (b, h) in units}
    q_dot_n = _head_sum(jnp.concatenate([qf[b] * n_x[b] for b in range(nb)], axis=0), bdh_ref[...])
    mt_u = {(b, h): mt[(b, h, 0)] * sub_f32[0] + mt[(b, h, 1)] * sub_f32[1] for (b, h) in units}
    yield

    outs = []
    for b in range(nb):
        mt_x = _join_halves(mt_u, b)
        inter_w = jnp.exp(bcum_x[b] + m_x[b] - mt_x)
        num = _join_halves(num_u, b) + inter_w * _join_halves(inter_q, b)
        den = _join_halves(den_u, b) + inter_w * _rows_of(q_dot_n, b, n)
        outs.append(num / jnp.maximum(jnp.abs(den), jnp.exp(-mt_x)))
        b_end = bcum_x[b][last:last + 1, :]
        w_log = b_end - bcum_x[b] + li_x[b]
        m_new = jnp.maximum(b_end + m_x[b], jnp.max(w_log, axis=0, keepdims=True))
        old_w = jnp.exp(b_end + m_x[b] - m_new)
        new_w = jnp.exp(w_log - m_new)
        k_w = (kf[b] * new_w).astype(BF16)
        for h in range(N_HALVES):
            sl = _lanes_of(h)
            c_ref[b, h] = old_w[:, sl] * cmat[(b, h)] + bdh_ref[...] * _mm_tn(k_w[:, sl], v[b][:, sl])
        n_ref[b] = old_w * n_x[b] + jnp.sum(new_w * kf[b], axis=0, keepdims=True)
        m_ref[b] = m_new
    if reverse:
        tot = jnp.concatenate([fwd_ref[b, rows, :] + outs[b] for b in range(nb)], axis=0)
        normed = tot * lax.rsqrt(_head_sum(tot * tot, bdh_ref[...]) * (1.0 / HEAD_DIM) + RMS_EPS)
        for b in range(nb):
            o_ref[b, rows, :] = (_sigmoid(o_gate_ref[b, rows, :]) * _rows_of(normed, b, n)
                                 * nw_ref[...]).astype(o_ref.dtype)
    else:
        for b in range(nb):
            o_ref[b, rows, :] = outs[b]


def _mlstm_operands(z3, i_bias, f_bias, norm_w, expand, bdh, rev, fwd):
    batch, seq, _ = z3.shape
    n = RET_CHUNK * RET_SUBCHUNKS
    nc = seq // n
    blk = lambda col: _seq_block(batch, n, GROUP, nc, rev, col)
    specs = [blk(ZD_Q), blk(ZD_K), blk(ZD_V), _seq_block(batch, n, LANES_V7X, nc, rev, Z_GATES_128),
             _const_block((1, LANES_V7X)), _const_block((1, LANES_V7X)),
             _const_block((LANES_V7X, GROUP)), _const_block((HALF, HALF))]
    operands = [z3, z3, z3, z3, i_bias, f_bias, expand[int(rev)], bdh]
    if rev:
        specs += [blk(ZD_O), blk(0), _const_block((1, GROUP))]
        operands += [z3, fwd, norm_w]
    return specs, operands


def _in_sequence(sections):
    for section in sections:
        yield from section


def _spaced(section, gap):
    while True:
        try:
            next(section)
        except StopIteration as stop:
            return stop.value
        yield
        yield from _pause(gap)


def _ret_mlstm_body(*refs, reverse, nb, n_ret_in):
    n_in = len(refs) - 6
    ret_o, mlstm_o, st_ref, c_ref, n_ref, m_ref = refs[n_in:]

    @pl.when(pl.program_id(0) == 0)
    def _():
        for ref in (st_ref, c_ref, n_ref, m_ref):
            ref[...] = jnp.zeros_like(ref)

    order = list(range(RET_SUBCHUNKS))
    if reverse:
        order.reverse()
    rows = lambda sub: slice(sub * RET_CHUNK, (sub + 1) * RET_CHUNK)
    _trace_interleaved(
        _in_sequence([_mlstm_section(refs[n_ret_in:n_in], mlstm_o, c_ref, n_ref, m_ref, rows(sub),
                                     reverse=reverse, nb=nb) for sub in order]),
        _in_sequence([_spaced(_ret_section(refs[:n_ret_in], ret_o, st_ref, rows(sub), reverse=reverse, nb=nb),
                              RET_STAGE_GAP) for sub in order]))


def _ret_mlstm(z3, cos_t, sin_t, log_gamma, i_bias, f_bias, norm_w, expand, bdh):
    batch, seq, _ = z3.shape
    n = RET_CHUNK * RET_SUBCHUNKS
    nc = seq // n
    scratch = [pltpu.VMEM((batch, N_HALVES, HALF, HALF), F32), pltpu.VMEM((batch, N_HALVES, HALF, HALF), F32),
               pltpu.VMEM((batch, 1, GROUP), F32), pltpu.VMEM((batch, 1, GROUP), F32)]
    partial_out = (None, None)
    for rev in (False, True):
        ret_specs, ret_ops = _ret_operands(z3, cos_t, sin_t, log_gamma, bdh, rev, partial_out[0])
        ml_specs, ml_ops = _mlstm_operands(z3, i_bias, f_bias, norm_w, expand, bdh, rev, partial_out[1])
        out_blk = _seq_block(batch, n, GROUP, nc, rev, 0)
        out_shape = jax.ShapeDtypeStruct((batch, seq, GROUP), BF16 if rev else F32)
        partial_out = pl.pallas_call(
            functools.partial(_ret_mlstm_body, reverse=rev, nb=batch, n_ret_in=len(ret_specs)),
            grid=(nc,), in_specs=ret_specs + ml_specs, out_specs=[out_blk, out_blk],
            out_shape=[out_shape, out_shape], scratch_shapes=scratch,
            compiler_params=_params("arbitrary"), name="ret_mlstm_bwd" if rev else "ret_mlstm_fwd",
        )(*ret_ops, *ml_ops)
    return partial_out


def _trace_all(gen):
    try:
        while True:
            next(gen)
    except StopIteration as stop:
        return stop.value


def _pause(stages):
    for _ in range(stages):
        yield


def _interleaved(main, side):
    live = [main] if side is None else [main, side]
    side_value = None
    while live:
        for gen in list(live):
            try:
                next(gen)
            except StopIteration as stop:
                live.remove(gen)
                if gen is side:
                    side_value = stop.value
            yield
    return side_value


def _trace_interleaved(main, side):
    return _trace_all(_interleaved(main, side))


def _rwkv_body(*refs, reverse, n_blocks, nb):
    if reverse:
        (z_ref, zp_ref, zn_ref, mu_ref, w0_ref, w2_ref, a0_ref, a2_ref, kk_ref, ka_ref, rk_ref,
         bdh_ref, g2_ref, lnw_ref, lnb_ref, wkvf_ref, bonf_ref, o_ref, s_ref) = refs
    else:
        (z_ref, zp_ref, zn_ref, mu_ref, w0_ref, w2_ref, a0_ref, a2_ref, kk_ref, ka_ref, rk_ref,
         bdh_ref, wkv_ref, bon_ref, s_ref) = refs
    step = pl.program_id(0)
    block = (n_blocks - 1 - step) if reverse else step

    @pl.when(step == 0)
    def _():
        s_ref[...] = jnp.zeros_like(s_ref)

    n = RWKV_CHUNK
    incl2, strict2 = _order_masks(n, reverse, copies=2)
    sub_masks = _sub_head_masks(BF16)
    units, chains = _units_and_chains(nb)

    def prepare(sub):
        return _rwkv_prepare(z_ref, zp_ref, zn_ref, sub, block, n_blocks, reverse, nb, mu_ref, w0_ref, w2_ref,
                             a0_ref, a2_ref, kk_ref, ka_ref, rk_ref, bdh_ref)

    def solve(prep, sub):
        rows = slice(sub * n, (sub + 1) * n)
        operand = lambda name, b, half: prep[name][b][:, _lanes_of(half)]
        state = {u: s_ref[u[0], u[1]] for u in units}
        xo = {u: _mm_nt(operand("ar", *u), state[u]) for u in units}
        yield
        ar_m = {c: operand("ar", c[0], c[1]) * sub_masks[c[2]] for c in chains}
        v_m = {c: operand("v", c[0], c[1]) * sub_masks[c[2]] for c in chains}
        bk = {u: jnp.concatenate([operand("b_t", *u), operand("k_t", *u)], axis=0) for u in units}
        scores = {c: _mm_nt(ar_m[c], bk[c[0:2]]) for c in chains}
        yield
        low = {c: jnp.where(strict2, scores[c][0:n], 0.0).astype(BF16) for c in chains}
        out_w = {c: jnp.where(incl2, scores[c][n:2 * n], 0.0).astype(BF16) for c in chains}
        zeros_n = jnp.zeros((n, HALF), BF16)
        lak_v = {c: _mm(low[c], jnp.concatenate([zeros_n, v_m[c]], axis=0)) for c in chains}
        x = {u: xo[u][0:n] + lak_v[u + (0,)] + lak_v[u + (1,)] for u in units}
        yield
        p = {c: low[c][:, 0:n] for c in chains}
        u_c = {c: x[c[0:2]] * sub_masks[c[2]].astype(F32) for c in chains}
        u_c = {c: u_c[c] + _mm(p[c], u_c[c]) for c in chains}
        yield
        span = 1
        while 2 * span < n:
            p = {c: _mm(p[c], p[c]).astype(BF16) for c in chains}
            yield
            u_c = {c: u_c[c] + _mm(p[c], u_c[c]) for c in chains}
            yield
            span *= 2
        o_c = {c: _mm(out_w[c], jnp.concatenate([u_c[c].astype(BF16), v_m[c]], axis=0)) for c in chains}
        yield
        for u in units:
            b, half = u
            u_all = u_c[u + (0,)] + u_c[u + (1,)]
            upd = (_mm_tn(u_all, operand("b_t", b, half))
                   + _mm_tn(operand("v", b, half), operand("k_t", b, half)))
            s_ref[b, half] = prep["gam_end"][b][:, _lanes_of(half)] * (state[u] + bdh_ref[...] * upd)
        yield
        wkv_u = {u: xo[u][n:2 * n] + o_c[u + (0,)] + o_c[u + (1,)] for u in units}
        if reverse:
            bdh = bdh_ref[...]
            wkv = jnp.concatenate([wkvf_ref[b, rows, :] + _join_halves(wkv_u, b) for b in range(nb)], axis=0)
            cen = wkv - _head_sum(wkv, bdh) * (1.0 / HEAD_DIM)
            var = _head_sum(cen * cen, bdh) * (1.0 / HEAD_DIM)
            gate = _mm(_sigmoid(prep["lat2"]), g2_ref[...])
            out = (cen * lax.rsqrt(var + RWKV_GN_EPS) * lnw_ref[...] + lnb_ref[...] + prep["bonus"]) * gate
            for b in range(nb):
                o_ref[b, rows, :] = (_rows_of(out, b, n)
                                     + bonf_ref[b, rows, :] * _rows_of(gate, b, n)).astype(o_ref.dtype)
        else:
            for b in range(nb):
                wkv_ref[b, rows, :] = _join_halves(wkv_u, b)
                bon_ref[b, rows, :] = _rows_of(prep["bonus"], b, n)

    order = list(range(RWKV_SUBCHUNKS))
    if reverse:
        order.reverse()
    prep = _trace_all(prepare(order[0]))
    for i, sub in enumerate(order):
        following = prepare(order[i + 1]) if i + 1 < len(order) else None
        prep = _trace_interleaved(solve(prep, sub), following)


def _rwkv_prepare(z_ref, zp_ref, zn_ref, sub, block, n_blocks, reverse, nb, mu_ref, w0_ref, w2_ref, a0_ref,
                  a2_ref, kk_ref, ka_ref, rk_ref, bdh_ref):
    n = RWKV_CHUNK
    lo, hi = sub * n, (sub + 1) * n
    ridx = lax.broadcasted_iota(jnp.int32, (n, 1), 0)
    shifted = []
    for b in range(nb):
        z = z_ref[b, lo:hi, :]
        if sub == 0:
            z_prev = jnp.where(block > 0, zp_ref[b, SUBLANES_V7X - 1:SUBLANES_V7X, :], 0.0)
        else:
            z_prev = z_ref[b, lo - 1:lo, :]
        if sub == RWKV_SUBCHUNKS - 1:
            z_next = jnp.where(block < n_blocks - 1, zn_ref[b, 0:1, :], 0.0)
        else:
            z_next = z_ref[b, hi:hi + 1, :]
        z_m1 = jnp.where(ridx == 0, z_prev, pltpu.roll(z, 1, 0))
        z_p1 = jnp.where(ridx == n - 1, z_next, pltpu.roll(z, n - 1, 0))
        shifted.append(z + mu_ref[...] * (0.5 * (z_m1 + z_p1) - z))
    zs = jnp.concatenate(shifted, axis=0)
    bdh = bdh_ref[...]
    r = zs[:, 0:GROUP]
    k = zs[:, GROUP:2 * GROUP]
    v = zs[:, 2 * GROUP:3 * GROUP]
    lat1 = zs[:, 3 * GROUP:3 * GROUP + LANES_V7X]
    lat2 = zs[:, 3 * GROUP + LANES_V7X:4 * GROUP]
    yield from _pause(RWKV_PREP_SPACING)
    w_pre = w0_ref[...] + _mm(jnp.tanh(lat1), w2_ref[...])
    log_decay = -jnp.exp(-_softplus(-w_pre) - 0.5)
    a = _sigmoid(a0_ref[...] + _mm(lat1, a2_ref[...]))
    kk_raw = k * kk_ref[...]
    yield from _pause(RWKV_PREP_SPACING)
    kk = kk_raw / jnp.maximum(jnp.sqrt(_head_sum(kk_raw * kk_raw, bdh)), 1e-12)
    kd = k * (1.0 + (a - 1.0) * ka_ref[...])
    yield from _pause(RWKV_PREP_SPACING)
    bonus = _head_sum(r * kd * rk_ref[...], bdh) * v
    yield from _pause(RWKV_PREP_SPACING)
    rows = nb * n
    row = lax.broadcasted_iota(jnp.int32, (rows, rows), 0)
    col = lax.broadcasted_iota(jnp.int32, (rows, rows), 1)
    same_seq = _div_pow2(row, n) == _div_pow2(col, n)
    before = (col >= row) if reverse else (col <= row)
    cum_incl = _sel_mm(same_seq & before, log_decay, terms=2)
    cum_excl = cum_incl - log_decay
    last = 0 if reverse else n - 1
    a_t = (-kk * jnp.exp(cum_excl)).astype(BF16)
    r_t = (r * jnp.exp(cum_incl)).astype(BF16)
    inv_g = jnp.exp(-cum_incl)
    b_t = (kk * a * inv_g).astype(BF16)
    k_t = (kd * inv_g).astype(BF16)
    vb = v.astype(BF16)
    per_seq = lambda t: [_rows_of(t, b, n) for b in range(nb)]
    return dict(ar=[jnp.concatenate([_rows_of(a_t, b, n), _rows_of(r_t, b, n)], axis=0) for b in range(nb)],
                b_t=per_seq(b_t), k_t=per_seq(k_t), v=per_seq(vb),
                gam_end=[jnp.exp(cum_incl[b * n + last:b * n + last + 1, :]) for b in range(nb)],
                bonus=bonus, lat2=lat2)


def _rwkv(z3, p, bdh):
    batch, seq, _ = z3.shape
    rows = RWKV_CHUNK * RWKV_SUBCHUNKS
    n_blocks = seq // rows
    rows8 = rows // SUBLANES_V7X
    last8 = seq // SUBLANES_V7X - 1
    wide = 4 * GROUP

    def halo(rev, offset):
        bidx = _chunk_index(n_blocks, rev)
        return pl.BlockSpec((batch, SUBLANES_V7X, wide), lambda c: (
            0, jnp.clip((bidx(c) + offset) * rows8 - (1 - offset), 0, last8), ZC_BLOCK_1024))

    vec = _const_block((1, GROUP))
    lora = _const_block((LANES_V7X, GROUP))
    out_blk = lambda rev: _seq_block(batch, rows, GROUP, n_blocks, rev, 0)
    out_shape = jax.ShapeDtypeStruct((batch, seq, GROUP), F32)

    def specs(rev):
        return [_seq_block(batch, rows, wide, n_blocks, rev, ZC_BLOCK_1024), halo(rev, 0), halo(rev, 1),
                _const_block((1, wide)), vec, lora, vec, lora, vec, vec, vec, _const_block((HALF, HALF))]

    common = dict(grid=(n_blocks,), scratch_shapes=[pltpu.VMEM((batch, N_HALVES, HALF, HALF), F32)],
                  compiler_params=_params("arbitrary"))
    wkv_f, bon_f = pl.pallas_call(
        functools.partial(_rwkv_body, reverse=False, n_blocks=n_blocks, nb=batch),
        in_specs=specs(False), out_specs=[out_blk(False), out_blk(False)],
        out_shape=[out_shape, out_shape], name="rwkv_fwd", **common,
    )(z3, z3, z3, p["mu"], p["w0"][0], p["w2"][0], p["a0"][0], p["a2"][0], p["k_k"], p["k_a"], p["r_k"], bdh)
    return pl.pallas_call(
        functools.partial(_rwkv_body, reverse=True, n_blocks=n_blocks, nb=batch),
        in_specs=specs(True) + [lora, vec, vec, out_blk(True), out_blk(True)],
        out_specs=out_blk(True), out_shape=jax.ShapeDtypeStruct((batch, seq, GROUP), BF16),
        name="rwkv_bwd", **common,
    )(z3, z3, z3, p["mu"], p["w0"][1], p["w2"][1], p["a0"][1], p["a2"][1], p["k_k"], p["k_a"], p["r_k"], bdh,
      p["g2"], p["ln_w"], p["ln_b"], wkv_f, bon_f)


def _outproj_body(x_ref, cb_ref, cc_ref, ch_ref, ccp_ref, chp_ref, ccn_ref, chn_ref, yb_ref, yc_ref, yd_ref,
                  cw_ref, cbias_ref, w_ref, g_ref, o_ref, *, tiles_per_seq):
    tm = x_ref.shape[0]
    t = pl.program_id(0) % tiles_per_seq
    u = cc_ref[...] * ch_ref[...]
    last8 = SUBLANES_V7X - 1
    u_prev = jnp.where(t > 0, ccp_ref[last8:last8 + 1, :] * chp_ref[last8:last8 + 1, :], 0.0)
    u_next = jnp.where(t < tiles_per_seq - 1, ccn_ref[0:1, :] * chn_ref[0:1, :], 0.0)
    ridx = lax.broadcasted_iota(jnp.int32, (tm, 1), 0)
    u_m1 = jnp.where(ridx == 0, u_prev, pltpu.roll(u, 1, 0))
    u_p1 = jnp.where(ridx == tm - 1, u_next, pltpu.roll(u, tm - 1, 0))
    cw = cw_ref[...]
    conv = cw[0:1, :] * u_m1 + cw[1:2, :] * u + cw[2:3, :] * u_p1 + cbias_ref[...]
    y_conv = cb_ref[...] * conv
    y = (_mm(y_conv, w_ref[0:GROUP, :]) + _mm(yb_ref[...], w_ref[GROUP:2 * GROUP, :])
         + _mm(yc_ref[...], w_ref[2 * GROUP:3 * GROUP, :]) + _mm(yd_ref[...], w_ref[3 * GROUP:4 * GROUP, :]))
    o_ref[...] = x_ref[...] + _rms_norm(y, g_ref[...])


def _outproj(x, z, y_ret, y_rwkv, y_mlstm, conv_w8, conv_b, w_out, layer, g, seq):
    n = x.shape[0]
    tm = min(OUTPROJ_ROW_TILE, seq)
    tiles_per_seq = seq // tm
    rows8 = tm // SUBLANES_V7X
    last8 = n // SUBLANES_V7X - 1
    zblk = lambda col: pl.BlockSpec((tm, GROUP), lambda i: (i, col))
    prev = lambda col: pl.BlockSpec((SUBLANES_V7X, GROUP), lambda i: (jnp.maximum(i * rows8 - 1, 0), col))
    nxt = lambda col: pl.BlockSpec((SUBLANES_V7X, GROUP), lambda i: (jnp.minimum((i + 1) * rows8, last8), col))
    yblk = pl.BlockSpec((tm, GROUP), lambda i: (i, 0))
    const = lambda i: (0, 0)
    return pl.pallas_call(
        functools.partial(_outproj_body, tiles_per_seq=tiles_per_seq),
        grid=(n // tm,),
        in_specs=[pl.BlockSpec((tm, D_MODEL), lambda i: (i, 0)),
                  zblk(ZA_B), zblk(ZA_C), zblk(ZA_H), prev(ZA_C), prev(ZA_H), nxt(ZA_C), nxt(ZA_H),
                  yblk, yblk, yblk,
                  pl.BlockSpec((SUBLANES_V7X, GROUP), const), pl.BlockSpec((1, GROUP), const),
                  pl.BlockSpec((None, D_MODEL, D_MODEL), lambda i: (layer, 0, 0)),
                  pl.BlockSpec((1, D_MODEL), const)],
        out_specs=pl.BlockSpec((tm, D_MODEL), lambda i: (i, 0)),
        out_shape=jax.ShapeDtypeStruct((n, D_MODEL), F32),
        compiler_params=_params("parallel"),
        name="outproj",
    )(x, z, z, z, z, z, z, z, y_ret, y_rwkv, y_mlstm, conv_w8, conv_b, w_out, g)


def _relayout_w_in(w):
    w = w.astype(BF16)
    n_conv, n_ret = 3 * GROUP, 4 * GROUP
    d0 = n_conv + n_ret + RWKV_COLS
    gates = w[:, d0 + 4 * GROUP:d0 + 4 * GROUP + 4 * N_HEADS]
    c_pad = 4 * GROUP - RWKV_COLS - 4 * N_HEADS
    assert RWKV_COLS - (4 * GROUP - LANES_V7X) == GATE_I_LANE and GATE_F_LANE == GATE_I_LANE + 2 * N_HEADS
    return jnp.concatenate([w[:, n_conv:d0], gates, jnp.zeros((D_MODEL, c_pad), BF16), w[:, 0:n_conv],
                            w[:, d0:d0 + 4 * GROUP]], axis=1)


def _block_diag_ones():
    idx = np.arange(HALF) // HEAD_DIM
    return jnp.asarray((idx[:, None] == idx[None, :]).astype(np.float32))


def _gate_expand():
    j = np.arange(LANES_V7X)[:, None]
    head = np.arange(GROUP)[None, :] // HEAD_DIM
    return jnp.asarray(np.stack([(j == GATE_F_LANE + d * N_HEADS + head) for d in range(2)]).astype(np.float32))


def _rope_tables(seq):
    inv = 10000.0 ** (-jnp.arange(0, HEAD_DIM, 2, dtype=F32) / HEAD_DIM)
    ang = jnp.arange(seq, dtype=F32)[:, None] * inv[None, :]
    cos, sin = jnp.cos(ang), jnp.sin(ang)
    cos_t = jnp.tile(jnp.concatenate([cos, cos], axis=1), (1, N_HEADS))
    sin_t = jnp.tile(jnp.concatenate([-sin, sin], axis=1), (1, N_HEADS))
    return cos_t, sin_t


def _lane_pad(v, width, start=0):
    v = v.reshape(1, -1)
    return jnp.pad(v, ((0, 0), (start, width - start - v.shape[1])))


def _rwkv_params(mu, w0, w2, a0, a2, g2, k_k, k_a, r_k, ln_w, ln_b):
    rank_w, rank_a, rank_g = w2.shape[1], a2.shape[1], g2.shape[0]
    w2p = jnp.zeros((2, LANES_V7X, GROUP), F32)
    a2p = jnp.zeros((2, LANES_V7X, GROUP), F32)
    for d in range(2):
        w2p = w2p.at[d, d * rank_w:(d + 1) * rank_w].set(w2[d])
        a2p = a2p.at[d, 2 * rank_w + d * rank_a:2 * rank_w + (d + 1) * rank_a].set(a2[d])
    g2p = jnp.zeros((LANES_V7X, GROUP), F32).at[0:rank_g].set(g2)
    return dict(mu=_lane_pad(mu, 4 * GROUP), w0=w0.reshape(2, 1, GROUP), w2=w2p.astype(BF16),
                a0=a0.reshape(2, 1, GROUP), a2=a2p.astype(BF16), g2=g2p.astype(BF16),
                k_k=k_k.reshape(1, GROUP), k_a=k_a.reshape(1, GROUP), r_k=r_k.reshape(1, GROUP),
                ln_w=ln_w.reshape(1, GROUP), ln_b=ln_b.reshape(1, GROUP))


def kernel(x, norm_g, ffn_w_gate, ffn_w_up, ffn_w_down, w_in, w_out, conv_w, conv_b, ret_decay_logit, rwkv_mu, rwkv_w0, rwkv_w2, rwkv_a0, rwkv_a2, rwkv_g2, rwkv_k_k, rwkv_k_a, rwkv_r_k, rwkv_ln_w, rwkv_ln_b, mlstm_i_bias, mlstm_f_bias, mlstm_norm_w):
    batch, seq, _ = x.shape
    depth = norm_g.shape[0]
    n_tok = batch * seq
    assert seq % OUTPROJ_ROW_TILE == 0 and seq % (RET_CHUNK * RET_SUBCHUNKS) == 0
    assert seq % (RWKV_CHUNK * RWKV_SUBCHUNKS) == 0
    cos_t, sin_t = _rope_tables(seq)
    bdh = _block_diag_ones()
    expand = _gate_expand()
    xf = x.reshape(n_tok, D_MODEL)
    ffn_w = (ffn_w_gate.astype(BF16), ffn_w_up.astype(BF16), ffn_w_down.astype(BF16))
    w_out_bf = w_out.astype(BF16)
    for l in range(depth):
        g = norm_g[l].reshape(6, 1, D_MODEL)
        xf = _ffn(xf, g[0], g[1], ffn_w, l, 0)
        z = _inproj(xf, g[2], _relayout_w_in(w_in[l]))
        z3 = z.reshape(batch, seq, Z_COLS)
        log_gamma = jax.nn.log_sigmoid(ret_decay_logit[l].astype(F32))
        y_ret, y_mlstm = _ret_mlstm(
            z3, cos_t, sin_t, log_gamma, _lane_pad(mlstm_i_bias[l], LANES_V7X, GATE_I_LANE),
            _lane_pad(mlstm_f_bias[l], LANES_V7X, GATE_F_LANE), mlstm_norm_w[l].reshape(1, GROUP), expand, bdh)
        y_rwkv = _rwkv(z3, _rwkv_params(
            rwkv_mu[l], rwkv_w0[l], rwkv_w2[l], rwkv_a0[l], rwkv_a2[l], rwkv_g2[l], rwkv_k_k[l],
            rwkv_k_a[l], rwkv_r_k[l], rwkv_ln_w[l], rwkv_ln_b[l]), bdh)
        conv_w8 = jnp.pad(conv_w[l], ((0, SUBLANES_V7X - conv_w.shape[1]), (0, 0)))
        xf = _outproj(xf, z, y_ret.reshape(n_tok, GROUP), y_rwkv.reshape(n_tok, GROUP),
                      y_mlstm.reshape(n_tok, GROUP), conv_w8, conv_b[l].reshape(1, GROUP),
                      w_out_bf, l, g[3], seq)
        xf = _ffn(xf, g[4], g[5], ffn_w, l, 1)
    return xf.reshape(batch, seq, D_MODEL)
```

```python
import functools

import numpy as np
import jax
import jax.numpy as jnp
from jax import lax
from jax.experimental import pallas as pl
from jax.experimental.pallas import tpu as pltpu

F32 = jnp.float32
BF16 = jnp.bfloat16

D_MODEL = 1024
GROUP = 256
N_HEADS = 4
HEAD_DIM = 64
D_FF = 2816
RWKV_COLS = 960
RMS_EPS = 1e-6
RWKV_GN_EPS = 64e-5

LANES_V7X = 128
SUBLANES_V7X = 8
VMEM_LIMIT_BYTES = 56 * 1024 * 1024

ROW_TILE = 512
OUTPROJ_ROW_TILE = 1024
FFN_CHUNK = 256
RET_CHUNK = 128
RET_SUBCHUNKS = 2
RET_STAGE_GAP = 1
RWKV_CHUNK = 64
RWKV_SUBCHUNKS = 4
RWKV_PREP_SPACING = 3

HALF = LANES_V7X
N_HALVES = GROUP // HALF
HEADS_PER_HALF = HALF // HEAD_DIM

Z_COLS = 3840
ZB_Q, ZB_K, ZB_V, ZB_G = 0, 1, 2, 3
ZC_BLOCK_1024 = 1
ZA_B, ZA_C, ZA_H = 8, 9, 10
ZD_Q, ZD_K, ZD_V, ZD_O = 11, 12, 13, 14
Z_GATES_128 = 15
GATE_I_LANE = 64
GATE_F_LANE = 72


def _mm(a, b):
    return jnp.dot(a.astype(BF16), b.astype(BF16), preferred_element_type=F32)


def _mm_nt(a, b):
    return lax.dot_general(a.astype(BF16), b.astype(BF16), (((1,), (1,)), ((), ())),
                           preferred_element_type=F32)


def _mm_tn(a, b):
    return lax.dot_general(a.astype(BF16), b.astype(BF16), (((0,), (0,)), ((), ())),
                           preferred_element_type=F32)


def _bf16_terms(x, terms):
    out = []
    for _ in range(terms - 1):
        part = x.astype(BF16)
        out.append(part)
        x = x - part.astype(F32)
    out.append(x.astype(BF16))
    return out


def _sel_mm(sel, x, terms=3):
    s = sel.astype(BF16)
    return sum(jnp.dot(s, part, preferred_element_type=F32) for part in _bf16_terms(x, terms))


def _mm_sel(x, sel, terms=3):
    s = sel.astype(BF16)
    return sum(jnp.dot(part, s, preferred_element_type=F32) for part in _bf16_terms(x, terms))


def _head_sum(x, bdh):
    return jnp.concatenate([_mm(x[:, _lanes_of(h)], bdh) for h in range(N_HALVES)], axis=1)


def _rows_of(stacked, b, n):
    return stacked[b * n:(b + 1) * n]


def _div_pow2(x, d):
    shift = d.bit_length() - 1
    assert 1 << shift == d
    return lax.shift_right_logical(x, shift)


def _rms_norm(x, g):
    return x * lax.rsqrt(jnp.mean(x * x, axis=-1, keepdims=True) + RMS_EPS) * g


def _softplus(x):
    return jnp.maximum(x, 0.0) + jnp.log(1.0 + jnp.exp(-jnp.abs(x)))


def _sigmoid(x):
    return 1.0 / (1.0 + jnp.exp(-x))


def _order_masks(n, reverse, copies=1):
    row = lax.broadcasted_iota(jnp.int32, (n, copies * n), 0)
    col = lax.broadcasted_iota(jnp.int32, (n, copies * n), 1)
    if copies > 1:
        assert n & (n - 1) == 0
        col = col & (n - 1)
    if reverse:
        return col >= row, col > row
    return col <= row, col < row


def _sub_head_masks(dtype):
    lane = lax.broadcasted_iota(jnp.int32, (1, HALF), 1)
    return [((lane >= j * HEAD_DIM) & (lane < (j + 1) * HEAD_DIM)).astype(dtype)
            for j in range(HEADS_PER_HALF)]


def _lanes_of(half):
    return slice(half * HALF, (half + 1) * HALF)


def _units_and_chains(nb):
    units = [(b, half) for b in range(nb) for half in range(N_HALVES)]
    chains = [(b, half, j) for (b, half) in units for j in range(HEADS_PER_HALF)]
    return units, chains


def _join_halves(per_unit, b):
    return jnp.concatenate([per_unit[(b, half)] for half in range(N_HALVES)], axis=1)


def _params(*sem):
    return pltpu.CompilerParams(dimension_semantics=sem, vmem_limit_bytes=VMEM_LIMIT_BYTES)


def _ffn_body(x_ref, gin_ref, gout_ref, wg_ref, wu_ref, wd_ref, *rest):
    if len(rest) == 2:
        o_ref, acc_ref = rest
    else:
        gproj_ref, win_ref, o_ref, z_ref, acc_ref = rest
    x = x_ref[...]
    h = _rms_norm(x, gin_ref[...]).astype(BF16)
    for c in range(D_FF // FFN_CHUNK):
        lo = c * FFN_CHUNK
        gate = jnp.dot(h, wg_ref[:, lo:lo + FFN_CHUNK], preferred_element_type=F32)
        up = jnp.dot(h, wu_ref[:, lo:lo + FFN_CHUNK], preferred_element_type=F32)
        act = (gate * _sigmoid(gate) * up).astype(BF16)
        part = jnp.dot(act, wd_ref[lo:lo + FFN_CHUNK, :], preferred_element_type=F32)
        if c == 0:
            acc_ref[...] = part
        else:
            acc_ref[...] += part
    x_new = x + 0.5 * _rms_norm(acc_ref[...], gout_ref[...])
    o_ref[...] = x_new
    if len(rest) > 2:
        z_ref[...] = jnp.dot(_rms_norm(x_new, gproj_ref[...]).astype(BF16), win_ref[...],
                             preferred_element_type=F32)


def _ffn(x, g_in, g_out, weights, layer, which, proj=None):
    w_gate, w_up, w_down = weights
    n = x.shape[0]
    tm = min(ROW_TILE, n)
    const = lambda i: (0, 0)
    pick = lambda i: (layer, which, 0, 0)
    rows = lambda width: pl.BlockSpec((tm, width), lambda i: (i, 0))
    in_specs = [rows(D_MODEL), pl.BlockSpec((1, D_MODEL), const), pl.BlockSpec((1, D_MODEL), const),
                pl.BlockSpec((None, None, D_MODEL, D_FF), pick), pl.BlockSpec((None, None, D_MODEL, D_FF), pick),
                pl.BlockSpec((None, None, D_FF, D_MODEL), pick)]
    operands = [x, g_in, g_out, w_gate, w_up, w_down]
    out_specs, out_shape = rows(D_MODEL), jax.ShapeDtypeStruct((n, D_MODEL), F32)
    if proj is not None:
        in_specs += [pl.BlockSpec((1, D_MODEL), const), pl.BlockSpec((D_MODEL, Z_COLS), const)]
        operands += list(proj)
        out_specs = [out_specs, rows(Z_COLS)]
        out_shape = [out_shape, jax.ShapeDtypeStruct((n, Z_COLS), F32)]
    return pl.pallas_call(
        _ffn_body, grid=(n // tm,), in_specs=in_specs, out_specs=out_specs, out_shape=out_shape,
        scratch_shapes=[pltpu.VMEM((tm, D_MODEL), F32)], compiler_params=_params("parallel"),
        name="ffn" if proj is None else "ffn_inproj",
    )(*operands)


def _chunk_index(nc, reverse):
    if reverse:
        return lambda c: nc - 1 - c
    return lambda c: c


def _seq_block(batch, rows, width, nc, reverse, col):
    cidx = _chunk_index(nc, reverse)
    return pl.BlockSpec((batch, rows, width), lambda c: (0, cidx(c), col))


def _const_block(shape):
    return pl.BlockSpec(shape, lambda c: (0,) * len(shape))


def _ret_section(in_refs, o_ref, st_ref, rows, *, reverse, nb):
    if reverse:
        lgs_ref, q_ref, k_ref, v_ref, cos_ref, sin_ref, lgv_ref, bdh_ref, g_ref, fwd_ref = in_refs
    else:
        lgs_ref, q_ref, k_ref, v_ref, cos_ref, sin_ref, lgv_ref, bdh_ref = in_refs
    n = RET_CHUNK
    lane = lax.broadcasted_iota(jnp.int32, (1, GROUP), 1)
    first_half = (lane % HEAD_DIM) < (HEAD_DIM // 2)
    cos = cos_ref[rows, :]
    sin = sin_ref[rows, :]

    def rot(t):
        swapped = jnp.where(first_half, pltpu.roll(t, GROUP - HEAD_DIM // 2, 1),
                            pltpu.roll(t, HEAD_DIM // 2, 1))
        return t * cos + swapped * sin

    lg = lgv_ref[...]
    ti = lax.broadcasted_iota(jnp.int32, (n, 1), 0).astype(F32)
    if reverse:
        q_exp, k_exp = n - ti, ti
    else:
        q_exp, k_exp = ti + 1.0, n - 1.0 - ti
    q_dec = jnp.exp(q_exp * lg)
    k_dec = jnp.exp(k_exp * lg)
    c_dec = jnp.exp(float(n) * lg)
    row = lax.broadcasted_iota(jnp.int32, (n, n), 0)
    col = lax.broadcasted_iota(jnp.int32, (n, n), 1)
    if reverse:
        rel, msk = (col - row).astype(F32), col > row
    else:
        rel, msk = (row - col).astype(F32), col <= row
    intra = [jnp.where(msk, jnp.exp(jnp.where(msk, rel, 0.0) * lgs_ref[h]), 0.0) for h in range(N_HEADS)]
    sub_masks = _sub_head_masks(BF16)
    units, chains = _units_and_chains(nb)

    q = [(rot(q_ref[b, rows, :]) * HEAD_DIM ** -0.5).astype(BF16) for b in range(nb)]
    k_rot = [rot(k_ref[b, rows, :]) for b in range(nb)]
    k = [t.astype(BF16) for t in k_rot]
    k_scaled = [(t * k_dec).astype(BF16) for t in k_rot]
    v = [v_ref[b, rows, :].astype(BF16) for b in range(nb)]
    yield
    state = {u: st_ref[u[0], u[1]] for u in units}
    inter = {(b, h): _mm(q[b][:, _lanes_of(h)], state[(b, h)]) for (b, h) in units}
    yield
    s = {(b, h, j): _mm_nt(q[b][:, _lanes_of(h)] * sub_masks[j], k[b][:, _lanes_of(h)])
         for (b, h, j) in chains}
    yield
    p = {(b, h, j): (s[(b, h, j)] * intra[h * HEADS_PER_HALF + j]).astype(BF16) for (b, h, j) in chains}
    intra_out = {
        (b, h): _mm(jnp.concatenate([p[(b, h, j)] for j in range(HEADS_PER_HALF)], axis=1),
                    jnp.concatenate([v[b][:, _lanes_of(h)] * sub_masks[j] for j in range(HEADS_PER_HALF)],
                                    axis=0))
        for (b, h) in units}
    yield
    for (b, h) in units:
        sl = _lanes_of(h)
        st_ref[b, h] = c_dec[:, sl] * state[(b, h)] + bdh_ref[...] * _mm_tn(k_scaled[b][:, sl], v[b][:, sl])
    yield
    outs = [q_dec * _join_halves(inter, b) + _join_halves(intra_out, b) for b in range(nb)]
    if reverse:
        tot = jnp.concatenate([fwd_ref[b, rows, :] + outs[b] for b in range(nb)], axis=0)
        normed = tot * lax.rsqrt(_head_sum(tot * tot, bdh_ref[...]) * (1.0 / HEAD_DIM) + RMS_EPS)
        for b in range(nb):
            g = g_ref[b, rows, :]
            o_ref[b, rows, :] = (g * _sigmoid(g) * _rows_of(normed, b, n)).astype(o_ref.dtype)
    else:
        for b in range(nb):
            o_ref[b, rows, :] = outs[b]


def _ret_operands(z3, cos_t, sin_t, log_gamma, bdh, rev, fwd):
    batch, seq, _ = z3.shape
    n = RET_CHUNK * RET_SUBCHUNKS
    nc = seq // n
    blk = lambda col: _seq_block(batch, n, GROUP, nc, rev, col)
    cidx = _chunk_index(nc, rev)
    tab = pl.BlockSpec((n, GROUP), lambda c: (cidx(c), 0))
    d = int(rev)
    lgv = jnp.repeat(log_gamma, HEAD_DIM, axis=1)
    specs = [pl.BlockSpec(memory_space=pltpu.SMEM), blk(ZB_Q), blk(ZB_K), blk(ZB_V), tab, tab,
             _const_block((1, GROUP)), _const_block((HALF, HALF))]
    operands = [log_gamma[d], z3, z3, z3, cos_t, sin_t, lgv[d:d + 1], bdh]
    if rev:
        specs += [blk(ZB_G), blk(0)]
        operands += [z3, fwd]
    return specs, operands


def _mlstm_section(in_refs, o_ref, c_ref, n_ref, m_ref, rows, *, reverse, nb):
    if reverse:
        (q_ref, k_ref, v_ref, gates_ref, ib_ref, fb_ref, ed_ref, bdh_ref, o_gate_ref, fwd_ref,
         nw_ref) = in_refs
    else:
        q_ref, k_ref, v_ref, gates_ref, ib_ref, fb_ref, ed_ref, bdh_ref = in_refs
    direction = int(reverse)
    n = RET_CHUNK
    incl, _ = _order_masks(n, reverse)
    sub_bf = _sub_head_masks(BF16)
    sub_f32 = _sub_head_masks(F32)
    units, chains = _units_and_chains(nb)
    ed = ed_ref[...]
    last = 0 if reverse else n - 1

    qf = [q_ref[b, rows, :] * HEAD_DIM ** -0.5 for b in range(nb)]
    q = [t.astype(BF16) for t in qf]
    kf = [k_ref[b, rows, :] for b in range(nb)]
    k = [t.astype(BF16) for t in kf]
    v = [v_ref[b, rows, :].astype(BF16) for b in range(nb)]
    gi = [pltpu.roll(gates_ref[b, rows, :] + ib_ref[...], GATE_F_LANE - GATE_I_LANE, 1) for b in range(nb)]
    lf = [-_softplus(-(gates_ref[b, rows, :] + fb_ref[...])) for b in range(nb)]
    yield
    bcum_lanes = _sel_mm(incl, jnp.concatenate(lf, axis=1))
    bcum = [bcum_lanes[:, b * LANES_V7X:(b + 1) * LANES_V7X] for b in range(nb)]
    yield
    row_terms = [(gi[b] - bcum[b]).T for b in range(nb)]
    li_all = _mm_sel(jnp.concatenate(gi, axis=0), ed)
    yield
    bcum_all = _mm_sel(jnp.concatenate(bcum, axis=0), ed)
    li_x = [_rows_of(li_all, b, n) for b in range(nb)]
    bcum_x = [_rows_of(bcum_all, b, n) for b in range(nb)]
    m_x = [m_ref[b] for b in range(nb)]
    n_x = [n_ref[b] for b in range(nb)]
    yield
    head_of = lambda c: c[1] * HEADS_PER_HALF + c[2]
    gate_lane = lambda c: GATE_F_LANE + direction * N_HEADS + head_of(c)
    colv = {c: bcum[c[0]][:, gate_lane(c):gate_lane(c) + 1] for c in chains}
    col_b = {c: jnp.broadcast_to(colv[c], (n, n)) for c in chains}
    yield
    d_log = {c: jnp.where(incl, col_b[c] + row_terms[c[0]][gate_lane(c):gate_lane(c) + 1, :], -jnp.inf)
             for c in chains}
    row_max = {c: jnp.max(d_log[c], axis=1, keepdims=True) for c in chains}
    yield
    mt = {c: jnp.maximum(colv[c] + m_x[c[0]][:, head_of(c) * HEAD_DIM:head_of(c) * HEAD_DIM + 1], row_max[c])
          for c in chains}
    mt_b = {c: jnp.broadcast_to(mt[c], (n, n)) for c in chains}
    yield
    dm = {c: jnp.exp(d_log[c] - mt_b[c]) for c in chains}
    cmat = {u: c_ref[u[0], u[1]] for u in units}
    s = {(b, h, j): (_mm_nt(q[b][:, _lanes_of(h)] * sub_bf[j], k[b][:, _lanes_of(h)]) * dm[(b, h, j)]
                     ).astype(BF16) for (b, h, j) in chains}
    yield
    ones_rows = _div_pow2(lax.broadcasted_iota(jnp.int32, (HEADS_PER_HALF * n, HALF), 0), n)
    ones_lanes = _div_pow2(lax.broadcasted_iota(jnp.int32, (HEADS_PER_HALF * n, HALF), 1), HEAD_DIM)
    head_ones = (ones_rows == ones_lanes).astype(BF16)
    num_u, den_u = {}, {}
    for (b, h) in units:
        v_h = v[b][:, _lanes_of(h)]
        rhs = jnp.concatenate([jnp.concatenate([v_h * sub_bf[j] for j in range(HEADS_PER_HALF)], axis=0),
                               head_ones], axis=1)
        nd = _mm(jnp.concatenate([s[(b, h, j)] for j in range(HEADS_PER_HALF)], axis=1), rhs)
        num_u[(b, h)] = nd[:, 0:HALF]
        den_u[(b, h)] = nd[:, HALF:2 * HALF]
    yield
    inter_q = {(b, h): _mm(q[b][:, _lanes_of(h)], cmat[(b, h)]) for (b, h) in units}
    q_dot_n = _head_sum(jnp.concatenate([qf[b] * n_x[b] for b in range(nb)], axis=0), bdh_ref[...])
    mt_u = {(b, h): mt[(b, h, 0)] * sub_f32[0] + mt[(b, h, 1)] * sub_f32[1] for (b, h) in units}
    yield

    outs = []
    for b in range(nb):
        mt_x = _join_halves(mt_u, b)
        inter_w = jnp.exp(bcum_x[b] + m_x[b] - mt_x)
        num = _join_halves(num_u, b) + inter_w * _join_halves(inter_q, b)
        den = _join_halves(den_u, b) + inter_w * _rows_of(q_dot_n, b, n)
        outs.append(num / jnp.maximum(jnp.abs(den), jnp.exp(-mt_x)))
        b_end = bcum_x[b][last:last + 1, :]
        w_log = b_end - bcum_x[b] + li_x[b]
        m_new = jnp.maximum(b_end + m_x[b], jnp.max(w_log, axis=0, keepdims=True))
        old_w = jnp.exp(b_end + m_x[b] - m_new)
        new_w = jnp.exp(w_log - m_new)
        k_w = (kf[b] * new_w).astype(BF16)
        for h in range(N_HALVES):
            sl = _lanes_of(h)
            c_ref[b, h] = old_w[:, sl] * cmat[(b, h)] + bdh_ref[...] * _mm_tn(k_w[:, sl], v[b][:, sl])
        n_ref[b] = old_w * n_x[b] + jnp.sum(new_w * kf[b], axis=0, keepdims=True)
        m_ref[b] = m_new
    if reverse:
        tot = jnp.concatenate([fwd_ref[b, rows, :] + outs[b] for b in range(nb)], axis=0)
        normed = tot * lax.rsqrt(_head_sum(tot * tot, bdh_ref[...]) * (1.0 / HEAD_DIM) + RMS_EPS)
        for b in range(nb):
            o_ref[b, rows, :] = (_sigmoid(o_gate_ref[b, rows, :]) * _rows_of(normed, b, n)
                                 * nw_ref[...]).astype(o_ref.dtype)
    else:
        for b in range(nb):
            o_ref[b, rows, :] = outs[b]


def _mlstm_operands(z3, i_bias, f_bias, norm_w, expand, bdh, rev, fwd):
    batch, seq, _ = z3.shape
    n = RET_CHUNK * RET_SUBCHUNKS
    nc = seq // n
    blk = lambda col: _seq_block(batch, n, GROUP, nc, rev, col)
    specs = [blk(ZD_Q), blk(ZD_K), blk(ZD_V), _seq_block(batch, n, LANES_V7X, nc, rev, Z_GATES_128),
             _const_block((1, LANES_V7X)), _const_block((1, LANES_V7X)),
             _const_block((LANES_V7X, GROUP)), _const_block((HALF, HALF))]
    operands = [z3, z3, z3, z3, i_bias, f_bias, expand[int(rev)], bdh]
    if rev:
        specs += [blk(ZD_O), blk(0), _const_block((1, GROUP))]
        operands += [z3, fwd, norm_w]
    return specs, operands


def _in_sequence(sections):
    for section in sections:
        yield from section


def _spaced(section, gap):
    while True:
        try:
            next(section)
        except StopIteration as stop:
            return stop.value
        yield
        yield from _pause(gap)


def _ret_mlstm_body(*refs, reverse, nb, n_ret_in):
    n_in = len(refs) - 6
    ret_o, mlstm_o, st_ref, c_ref, n_ref, m_ref = refs[n_in:]

    @pl.when(pl.program_id(0) == 0)
    def _():
        for ref in (st_ref, c_ref, n_ref, m_ref):
            ref[...] = jnp.zeros_like(ref)

    order = list(range(RET_SUBCHUNKS))
    if reverse:
        order.reverse()
    rows = lambda sub: slice(sub * RET_CHUNK, (sub + 1) * RET_CHUNK)
    _trace_interleaved(
        _in_sequence([_mlstm_section(refs[n_ret_in:n_in], mlstm_o, c_ref, n_ref, m_ref, rows(sub),
                                     reverse=reverse, nb=nb) for sub in order]),
        _in_sequence([_spaced(_ret_section(refs[:n_ret_in], ret_o, st_ref, rows(sub), reverse=reverse, nb=nb),
                              RET_STAGE_GAP) for sub in order]))


def _ret_mlstm(z3, cos_t, sin_t, log_gamma, i_bias, f_bias, norm_w, expand, bdh):
    batch, seq, _ = z3.shape
    n = RET_CHUNK * RET_SUBCHUNKS
    nc = seq // n
    scratch = [pltpu.VMEM((batch, N_HALVES, HALF, HALF), F32), pltpu.VMEM((batch, N_HALVES, HALF, HALF), F32),
               pltpu.VMEM((batch, 1, GROUP), F32), pltpu.VMEM((batch, 1, GROUP), F32)]
    partial_out = (None, None)
    for rev in (False, True):
        ret_specs, ret_ops = _ret_operands(z3, cos_t, sin_t, log_gamma, bdh, rev, partial_out[0])
        ml_specs, ml_ops = _mlstm_operands(z3, i_bias, f_bias, norm_w, expand, bdh, rev, partial_out[1])
        out_blk = _seq_block(batch, n, GROUP, nc, rev, 0)
        out_shape = jax.ShapeDtypeStruct((batch, seq, GROUP), BF16 if rev else F32)
        partial_out = pl.pallas_call(
            functools.partial(_ret_mlstm_body, reverse=rev, nb=batch, n_ret_in=len(ret_specs)),
            grid=(nc,), in_specs=ret_specs + ml_specs, out_specs=[out_blk, out_blk],
            out_shape=[out_shape, out_shape], scratch_shapes=scratch,
            compiler_params=_params("arbitrary"), name="ret_mlstm_bwd" if rev else "ret_mlstm_fwd",
        )(*ret_ops, *ml_ops)
    return partial_out


def _trace_all(gen):
    try:
        while True:
            next(gen)
    except StopIteration as stop:
        return stop.value


def _pause(stages):
    for _ in range(stages):
        yield


def _interleaved(main, side):
    live = [main] if side is None else [main, side]
    side_value = None
    while live:
        for gen in list(live):
            try:
                next(gen)
            except StopIteration as stop:
                live.remove(gen)
                if gen is side:
                    side_value = stop.value
            yield
    return side_value


def _trace_interleaved(main, side):
    return _trace_all(_interleaved(main, side))


def _rwkv_body(*refs, reverse, n_blocks, nb):
    if reverse:
        (z_ref, zp_ref, zn_ref, mu_ref, w0_ref, w2_ref, a0_ref, a2_ref, kk_ref, ka_ref, rk_ref,
         bdh_ref, g2_ref, lnw_ref, lnb_ref, wkvf_ref, bonf_ref, o_ref, s_ref) = refs
    else:
        (z_ref, zp_ref, zn_ref, mu_ref, w0_ref, w2_ref, a0_ref, a2_ref, kk_ref, ka_ref, rk_ref,
         bdh_ref, wkv_ref, bon_ref, s_ref) = refs
    step = pl.program_id(0)
    block = (n_blocks - 1 - step) if reverse else step

    @pl.when(step == 0)
    def _():
        s_ref[...] = jnp.zeros_like(s_ref)

    n = RWKV_CHUNK
    incl2, strict2 = _order_masks(n, reverse, copies=2)
    sub_masks = _sub_head_masks(BF16)
    units, chains = _units_and_chains(nb)

    def prepare(sub):
        return _rwkv_prepare(z_ref, zp_ref, zn_ref, sub, block, n_blocks, reverse, nb, mu_ref, w0_ref, w2_ref,
                             a0_ref, a2_ref, kk_ref, ka_ref, rk_ref, bdh_ref)

    def solve(prep, sub):
        rows = slice(sub * n, (sub + 1) * n)
        operand = lambda name, b, half: prep[name][b][:, _lanes_of(half)]
        state = {u: s_ref[u[0], u[1]] for u in units}
        xo = {u: _mm_nt(operand("ar", *u), state[u]) for u in units}
        yield
        ar_m = {c: operand("ar", c[0], c[1]) * sub_masks[c[2]] for c in chains}
        v_m = {c: operand("v", c[0], c[1]) * sub_masks[c[2]] for c in chains}
        bk = {u: jnp.concatenate([operand("b_t", *u), operand("k_t", *u)], axis=0) for u in units}
        scores = {c: _mm_nt(ar_m[c], bk[c[0:2]]) for c in chains}
        yield
        low = {c: jnp.where(strict2, scores[c][0:n], 0.0).astype(BF16) for c in chains}
        out_w = {c: jnp.where(incl2, scores[c][n:2 * n], 0.0).astype(BF16) for c in chains}
        zeros_n = jnp.zeros((n, HALF), BF16)
        lak_v = {c: _mm(low[c], jnp.concatenate([zeros_n, v_m[c]], axis=0)) for c in chains}
        x = {u: xo[u][0:n] + lak_v[u + (0,)] + lak_v[u + (1,)] for u in units}
        yield
        p = {c: low[c][:, 0:n] for c in chains}
        u_c = {c: x[c[0:2]] * sub_masks[c[2]].astype(F32) for c in chains}
        u_c = {c: u_c[c] + _mm(p[c], u_c[c]) for c in chains}
        yield
        span = 1
        while 2 * span < n:
            p = {c: _mm(p[c], p[c]).astype(BF16) for c in chains}
            yield
            u_c = {c: u_c[c] + _mm(p[c], u_c[c]) for c in chains}
            yield
            span *= 2
        o_c = {c: _mm(out_w[c], jnp.concatenate([u_c[c].astype(BF16), v_m[c]], axis=0)) for c in chains}
        yield
        for u in units:
            b, half = u
            u_all = u_c[u + (0,)] + u_c[u + (1,)]
            upd = (_mm_tn(u_all, operand("b_t", b, half))
                   + _mm_tn(operand("v", b, half), operand("k_t", b, half)))
            s_ref[b, half] = prep["gam_end"][b][:, _lanes_of(half)] * (state[u] + bdh_ref[...] * upd)
        yield
        wkv_u = {u: xo[u][n:2 * n] + o_c[u + (0,)] + o_c[u + (1,)] for u in units}
        if reverse:
            bdh = bdh_ref[...]
            wkv = jnp.concatenate([wkvf_ref[b, rows, :] + _join_halves(wkv_u, b) for b in range(nb)], axis=0)
            cen = wkv - _head_sum(wkv, bdh) * (1.0 / HEAD_DIM)
            var = _head_sum(cen * cen, bdh) * (1.0 / HEAD_DIM)
            gate = _mm(_sigmoid(prep["lat2"]), g2_ref[...])
            out = (cen * lax.rsqrt(var + RWKV_GN_EPS) * lnw_ref[...] + lnb_ref[...] + prep["bonus"]) * gate
            for b in range(nb):
                o_ref[b, rows, :] = (_rows_of(out, b, n)
                                     + bonf_ref[b, rows, :] * _rows_of(gate, b, n)).astype(o_ref.dtype)
        else:
            for b in range(nb):
                wkv_ref[b, rows, :] = _join_halves(wkv_u, b)
                bon_ref[b, rows, :] = _rows_of(prep["bonus"], b, n)

    order = list(range(RWKV_SUBCHUNKS))
    if reverse:
        order.reverse()
    prep = _trace_all(prepare(order[0]))
    for i, sub in enumerate(order):
        following = prepare(order[i + 1]) if i + 1 < len(order) else None
        prep = _trace_interleaved(solve(prep, sub), following)


def _rwkv_prepare(z_ref, zp_ref, zn_ref, sub, block, n_blocks, reverse, nb, mu_ref, w0_ref, w2_ref, a0_ref,
                  a2_ref, kk_ref, ka_ref, rk_ref, bdh_ref):
    n = RWKV_CHUNK
    lo, hi = sub * n, (sub + 1) * n
    ridx = lax.broadcasted_iota(jnp.int32, (n, 1), 0)
    shifted = []
    for b in range(nb):
        z = z_ref[b, lo:hi, :]
        if sub == 0:
            z_prev = jnp.where(block > 0, zp_ref[b, SUBLANES_V7X - 1:SUBLANES_V7X, :], 0.0)
        else:
            z_prev = z_ref[b, lo - 1:lo, :]
        if sub == RWKV_SUBCHUNKS - 1:
            z_next = jnp.where(block < n_blocks - 1, zn_ref[b, 0:1, :], 0.0)
        else:
            z_next = z_ref[b, hi:hi + 1, :]
        z_m1 = jnp.where(ridx == 0, z_prev, pltpu.roll(z, 1, 0))
        z_p1 = jnp.where(ridx == n - 1, z_next, pltpu.roll(z, n - 1, 0))
        shifted.append(z + mu_ref[...] * (0.5 * (z_m1 + z_p1) - z))
    zs = jnp.concatenate(shifted, axis=0)
    bdh = bdh_ref[...]
    r = zs[:, 0:GROUP]
    k = zs[:, GROUP:2 * GROUP]
    v = zs[:, 2 * GROUP:3 * GROUP]
    lat1 = zs[:, 3 * GROUP:3 * GROUP + LANES_V7X]
    lat2 = zs[:, 3 * GROUP + LANES_V7X:4 * GROUP]
    yield from _pause(RWKV_PREP_SPACING)
    w_pre = w0_ref[...] + _mm(jnp.tanh(lat1), w2_ref[...])
    log_decay = -jnp.exp(-_softplus(-w_pre) - 0.5)
    a = _sigmoid(a0_ref[...] + _mm(lat1, a2_ref[...]))
    kk_raw = k * kk_ref[...]
    yield from _pause(RWKV_PREP_SPACING)
    kk = kk_raw / jnp.maximum(jnp.sqrt(_head_sum(kk_raw * kk_raw, bdh)), 1e-12)
    kd = k * (1.0 + (a - 1.0) * ka_ref[...])
    yield from _pause(RWKV_PREP_SPACING)
    bonus = _head_sum(r * kd * rk_ref[...], bdh) * v
    yield from _pause(RWKV_PREP_SPACING)
    rows = nb * n
    row = lax.broadcasted_iota(jnp.int32, (rows, rows), 0)
    col = lax.broadcasted_iota(jnp.int32, (rows, rows), 1)
    same_seq = _div_pow2(row, n) == _div_pow2(col, n)
    before = (col >= row) if reverse else (col <= row)
    cum_incl = _sel_mm(same_seq & before, log_decay, terms=2)
    cum_excl = cum_incl - log_decay
    last = 0 if reverse else n - 1
    a_t = (-kk * jnp.exp(cum_excl)).astype(BF16)
    r_t = (r * jnp.exp(cum_incl)).astype(BF16)
    inv_g = jnp.exp(-cum_incl)
    b_t = (kk * a * inv_g).astype(BF16)
    k_t = (kd * inv_g).astype(BF16)
    vb = v.astype(BF16)
    per_seq = lambda t: [_rows_of(t, b, n) for b in range(nb)]
    return dict(ar=[jnp.concatenate([_rows_of(a_t, b, n), _rows_of(r_t, b, n)], axis=0) for b in range(nb)],
                b_t=per_seq(b_t), k_t=per_seq(k_t), v=per_seq(vb),
                gam_end=[jnp.exp(cum_incl[b * n + last:b * n + last + 1, :]) for b in range(nb)],
                bonus=bonus, lat2=lat2)


def _rwkv(z3, p, bdh):
    batch, seq, _ = z3.shape
    rows = RWKV_CHUNK * RWKV_SUBCHUNKS
    n_blocks = seq // rows
    rows8 = rows // SUBLANES_V7X
    last8 = seq // SUBLANES_V7X - 1
    wide = 4 * GROUP

    def halo(rev, offset):
        bidx = _chunk_index(n_blocks, rev)
        return pl.BlockSpec((batch, SUBLANES_V7X, wide), lambda c: (
            0, jnp.clip((bidx(c) + offset) * rows8 - (1 - offset), 0, last8), ZC_BLOCK_1024))

    vec = _const_block((1, GROUP))
    lora = _const_block((LANES_V7X, GROUP))
    out_blk = lambda rev: _seq_block(batch, rows, GROUP, n_blocks, rev, 0)
    out_shape = jax.ShapeDtypeStruct((batch, seq, GROUP), F32)

    def specs(rev):
        return [_seq_block(batch, rows, wide, n_blocks, rev, ZC_BLOCK_1024), halo(rev, 0), halo(rev, 1),
                _const_block((1, wide)), vec, lora, vec, lora, vec, vec, vec, _const_block((HALF, HALF))]

    common = dict(grid=(n_blocks,), scratch_shapes=[pltpu.VMEM((batch, N_HALVES, HALF, HALF), F32)],
                  compiler_params=_params("arbitrary"))
    wkv_f, bon_f = pl.pallas_call(
        functools.partial(_rwkv_body, reverse=False, n_blocks=n_blocks, nb=batch),
        in_specs=specs(False), out_specs=[out_blk(False), out_blk(False)],
        out_shape=[out_shape, out_shape], name="rwkv_fwd", **common,
    )(z3, z3, z3, p["mu"], p["w0"][0], p["w2"][0], p["a0"][0], p["a2"][0], p["k_k"], p["k_a"], p["r_k"], bdh)
    return pl.pallas_call(
        functools.partial(_rwkv_body, reverse=True, n_blocks=n_blocks, nb=batch),
        in_specs=specs(True) + [lora, vec, vec, out_blk(True), out_blk(True)],
        out_specs=out_blk(True), out_shape=jax.ShapeDtypeStruct((batch, seq, GROUP), BF16),
        name="rwkv_bwd", **common,
    )(z3, z3, z3, p["mu"], p["w0"][1], p["w2"][1], p["a0"][1], p["a2"][1], p["k_k"], p["k_a"], p["r_k"], bdh,
      p["g2"], p["ln_w"], p["ln_b"], wkv_f, bon_f)


def _outproj_body(x_ref, cb_ref, cc_ref, ch_ref, ccp_ref, chp_ref, ccn_ref, chn_ref, yb_ref, yc_ref, yd_ref,
                  cw_ref, cbias_ref, w_ref, g_ref, o_ref, *, tiles_per_seq):
    tm = x_ref.shape[0]
    t = pl.program_id(0) % tiles_per_seq
    u = cc_ref[...] * ch_ref[...]
    last8 = SUBLANES_V7X - 1
    u_prev = jnp.where(t > 0, ccp_ref[last8:last8 + 1, :] * chp_ref[last8:last8 + 1, :], 0.0)
    u_next = jnp.where(t < tiles_per_seq - 1, ccn_ref[0:1, :] * chn_ref[0:1, :], 0.0)
    ridx = lax.broadcasted_iota(jnp.int32, (tm, 1), 0)
    u_m1 = jnp.where(ridx == 0, u_prev, pltpu.roll(u, 1, 0))
    u_p1 = jnp.where(ridx == tm - 1, u_next, pltpu.roll(u, tm - 1, 0))
    cw = cw_ref[...]
    conv = cw[0:1, :] * u_m1 + cw[1:2, :] * u + cw[2:3, :] * u_p1 + cbias_ref[...]
    y_conv = cb_ref[...] * conv
    y = (_mm(y_conv, w_ref[0:GROUP, :]) + _mm(yb_ref[...], w_ref[GROUP:2 * GROUP, :])
         + _mm(yc_ref[...], w_ref[2 * GROUP:3 * GROUP, :]) + _mm(yd_ref[...], w_ref[3 * GROUP:4 * GROUP, :]))
    o_ref[...] = x_ref[...] + _rms_norm(y, g_ref[...])


def _outproj(x, z, y_ret, y_rwkv, y_mlstm, conv_w8, conv_b, w_out, layer, g, seq):
    n = x.shape[0]
    tm = min(OUTPROJ_ROW_TILE, seq)
    tiles_per_seq = seq // tm
    rows8 = tm // SUBLANES_V7X
    last8 = n // SUBLANES_V7X - 1
    zblk = lambda col: pl.BlockSpec((tm, GROUP), lambda i: (i, col))
    prev = lambda col: pl.BlockSpec((SUBLANES_V7X, GROUP), lambda i: (jnp.maximum(i * rows8 - 1, 0), col))
    nxt = lambda col: pl.BlockSpec((SUBLANES_V7X, GROUP), lambda i: (jnp.minimum((i + 1) * rows8, last8), col))
    yblk = pl.BlockSpec((tm, GROUP), lambda i: (i, 0))
    const = lambda i: (0, 0)
    return pl.pallas_call(
        functools.partial(_outproj_body, tiles_per_seq=tiles_per_seq),
        grid=(n // tm,),
        in_specs=[pl.BlockSpec((tm, D_MODEL), lambda i: (i, 0)),
                  zblk(ZA_B), zblk(ZA_C), zblk(ZA_H), prev(ZA_C), prev(ZA_H), nxt(ZA_C), nxt(ZA_H),
                  yblk, yblk, yblk,
                  pl.BlockSpec((SUBLANES_V7X, GROUP), const), pl.BlockSpec((1, GROUP), const),
                  pl.BlockSpec((None, D_MODEL, D_MODEL), lambda i: (layer, 0, 0)),
                  pl.BlockSpec((1, D_MODEL), const)],
        out_specs=pl.BlockSpec((tm, D_MODEL), lambda i: (i, 0)),
        out_shape=jax.ShapeDtypeStruct((n, D_MODEL), F32),
        compiler_params=_params("parallel"),
        name="outproj",
    )(x, z, z, z, z, z, z, z, y_ret, y_rwkv, y_mlstm, conv_w8, conv_b, w_out, g)


def _relayout_w_in(w):
    w = w.astype(BF16)
    n_conv, n_ret = 3 * GROUP, 4 * GROUP
    d0 = n_conv + n_ret + RWKV_COLS
    gates = w[:, d0 + 4 * GROUP:d0 + 4 * GROUP + 4 * N_HEADS]
    c_pad = 4 * GROUP - RWKV_COLS - 4 * N_HEADS
    assert RWKV_COLS - (4 * GROUP - LANES_V7X) == GATE_I_LANE and GATE_F_LANE == GATE_I_LANE + 2 * N_HEADS
    return jnp.concatenate([w[:, n_conv:d0], gates, jnp.zeros((D_MODEL, c_pad), BF16), w[:, 0:n_conv],
                            w[:, d0:d0 + 4 * GROUP]], axis=1)


def _block_diag_ones():
    idx = np.arange(HALF) // HEAD_DIM
    return jnp.asarray((idx[:, None] == idx[None, :]).astype(np.float32))


def _gate_expand():
    j = np.arange(LANES_V7X)[:, None]
    head = np.arange(GROUP)[None, :] // HEAD_DIM
    return jnp.asarray(np.stack([(j == GATE_F_LANE + d * N_HEADS + head) for d in range(2)]).astype(np.float32))


def _rope_tables(seq):
    inv = 10000.0 ** (-jnp.arange(0, HEAD_DIM, 2, dtype=F32) / HEAD_DIM)
    ang = jnp.arange(seq, dtype=F32)[:, None] * inv[None, :]
    cos, sin = jnp.cos(ang), jnp.sin(ang)
    cos_t = jnp.tile(jnp.concatenate([cos, cos], axis=1), (1, N_HEADS))
    sin_t = jnp.tile(jnp.concatenate([-sin, sin], axis=1), (1, N_HEADS))
    return cos_t, sin_t


def _lane_pad(v, width, start=0):
    v = v.reshape(1, -1)
    return jnp.pad(v, ((0, 0), (start, width - start - v.shape[1])))


def _rwkv_params(mu, w0, w2, a0, a2, g2, k_k, k_a, r_k, ln_w, ln_b):
    rank_w, rank_a, rank_g = w2.shape[1], a2.shape[1], g2.shape[0]
    w2p = jnp.zeros((2, LANES_V7X, GROUP), F32)
    a2p = jnp.zeros((2, LANES_V7X, GROUP), F32)
    for d in range(2):
        w2p = w2p.at[d, d * rank_w:(d + 1) * rank_w].set(w2[d])
        a2p = a2p.at[d, 2 * rank_w + d * rank_a:2 * rank_w + (d + 1) * rank_a].set(a2[d])
    g2p = jnp.zeros((LANES_V7X, GROUP), F32).at[0:rank_g].set(g2)
    return dict(mu=_lane_pad(mu, 4 * GROUP), w0=w0.reshape(2, 1, GROUP), w2=w2p.astype(BF16),
                a0=a0.reshape(2, 1, GROUP), a2=a2p.astype(BF16), g2=g2p.astype(BF16),
                k_k=k_k.reshape(1, GROUP), k_a=k_a.reshape(1, GROUP), r_k=r_k.reshape(1, GROUP),
                ln_w=ln_w.reshape(1, GROUP), ln_b=ln_b.reshape(1, GROUP))


def kernel(x, norm_g, ffn_w_gate, ffn_w_up, ffn_w_down, w_in, w_out, conv_w, conv_b, ret_decay_logit, rwkv_mu, rwkv_w0, rwkv_w2, rwkv_a0, rwkv_a2, rwkv_g2, rwkv_k_k, rwkv_k_a, rwkv_r_k, rwkv_ln_w, rwkv_ln_b, mlstm_i_bias, mlstm_f_bias, mlstm_norm_w):
    batch, seq, _ = x.shape
    depth = norm_g.shape[0]
    n_tok = batch * seq
    assert seq % OUTPROJ_ROW_TILE == 0 and seq % (RET_CHUNK * RET_SUBCHUNKS) == 0
    assert seq % (RWKV_CHUNK * RWKV_SUBCHUNKS) == 0
    cos_t, sin_t = _rope_tables(seq)
    bdh = _block_diag_ones()
    expand = _gate_expand()
    xf = x.reshape(n_tok, D_MODEL)
    ffn_w = (ffn_w_gate.astype(BF16), ffn_w_up.astype(BF16), ffn_w_down.astype(BF16))
    w_out_bf = w_out.astype(BF16)
    for l in range(depth):
        g = norm_g[l].reshape(6, 1, D_MODEL)
        xf, z = _ffn(xf, g[0], g[1], ffn_w, l, 0, proj=(g[2], _relayout_w_in(w_in[l])))
        z3 = z.reshape(batch, seq, Z_COLS)
        log_gamma = jax.nn.log_sigmoid(ret_decay_logit[l].astype(F32))
        y_ret, y_mlstm = _ret_mlstm(
            z3, cos_t, sin_t, log_gamma, _lane_pad(mlstm_i_bias[l], LANES_V7X, GATE_I_LANE),
            _lane_pad(mlstm_f_bias[l], LANES_V7X, GATE_F_LANE), mlstm_norm_w[l].reshape(1, GROUP), expand, bdh)
        y_rwkv = _rwkv(z3, _rwkv_params(
            rwkv_mu[l], rwkv_w0[l], rwkv_w2[l], rwkv_a0[l], rwkv_a2[l], rwkv_g2[l], rwkv_k_k[l],
            rwkv_k_a[l], rwkv_r_k[l], rwkv_ln_w[l], rwkv_ln_b[l]), bdh)
        conv_w8 = jnp.pad(conv_w[l], ((0, SUBLANES_V7X - conv_w.shape[1]), (0, 0)))
        xf = _outproj(xf, z, y_ret.reshape(n_tok, GROUP), y_rwkv.reshape(n_tok, GROUP),
                      y_mlstm.reshape(n_tok, GROUP), conv_w8, conv_b[l].reshape(1, GROUP),
                      w_out_bf, l, g[3], seq)
        xf = _ffn(xf, g[4], g[5], ffn_w, l, 1)
    return xf.reshape(batch, seq, D_MODEL)
```
